```python
import math
import jax, jax.numpy as jnp
from jax import lax
import numpy as np

D_MODEL = 4096
BATCH = 2
SEQ = 8192
DEPTH = 4

HEAD_DIM = 128
N_HEADS = D_MODEL // HEAD_DIM
N_KV_HEADS = N_HEADS // 4
GROUP = N_HEADS // N_KV_HEADS
Q_W = N_HEADS * HEAD_DIM
KV_W = N_KV_HEADS * HEAD_DIM
QKV_W = Q_W + 2 * KV_W

N_MIXERS = 4
LAYERS_PER_MIXER = tuple(len(range(m, DEPTH, N_MIXERS)) for m in range(N_MIXERS))
Q_BLOCK = 128

T5_BUCKETS = 32
T5_MAX_DIST = 2048

DSA_TOPK = 256
IDX_HEADS = 32
IDX_DIM = 128
DSA_IN_W = QKV_W + IDX_HEADS * IDX_DIM + IDX_DIM + IDX_HEADS

SWA_WINDOW = 128

MOBA_BLOCK = 256
MOBA_TOPK = 3
MOBA_Q_CHUNK = 16

DILATED_BRANCHES = ((128, 1), (512, 4), (2048, 16))

D_FF = (D_MODEL * 43) // 16
FFN_CONV_WIDTH = 3

RMS_EPS = 1e-6

kernel_name = "hybrid_dsa_swa_moba_dilated_trunk"


def rms_norm(x, g):
    x32 = x.astype(jnp.float32)
    y = x32 * lax.rsqrt(jnp.mean(x32 * x32, axis=-1, keepdims=True) + RMS_EPS)
    return y.astype(x.dtype) * g


def t5_bucket(dist):
    n = jnp.maximum(dist, 0)
    max_exact = T5_BUCKETS // 2
    nf = jnp.maximum(n, 1).astype(jnp.float32)
    large = max_exact + (jnp.log(nf / max_exact) / math.log(T5_MAX_DIST / max_exact)
                         * (T5_BUCKETS - max_exact)).astype(jnp.int32)
    large = jnp.minimum(large, T5_BUCKETS - 1)
    return jnp.where(n < max_exact, n, large)


def t5_bias(table, dist):
    b = table[t5_bucket(dist)].astype(jnp.float32)
    return b.reshape(dist.shape + (N_KV_HEADS, GROUP))


def split_qkv(p, T):
    B = p.shape[0]
    q = p[..., :Q_W].reshape(B, T, N_KV_HEADS, GROUP, HEAD_DIM)
    k = p[..., Q_W:Q_W + KV_W].reshape(B, T, N_KV_HEADS, HEAD_DIM)
    v = p[..., Q_W + KV_W:QKV_W].reshape(B, T, N_KV_HEADS, HEAD_DIM)
    return q, k, v


def map_query_blocks(fn, n_queries, block):
    starts = jnp.arange(n_queries // block, dtype=jnp.int32) * block
    y = lax.map(fn, starts)
    nb, b, _, w = y.shape
    return jnp.swapaxes(y, 0, 1).reshape(b, nb * block, w)


def dsa_mixer(h, table, w_in, w_out):
    B, T, _ = h.shape
    k_sel = min(DSA_TOPK, T // 4)
    p = h @ w_in
    q, k, v = split_qkv(p[..., :QKV_W], T)
    o = QKV_W
    iq = p[..., o:o + IDX_HEADS * IDX_DIM].reshape(B, T, IDX_HEADS, IDX_DIM)
    o += IDX_HEADS * IDX_DIM
    ik = p[..., o:o + IDX_DIM]
    o += IDX_DIM
    iw = p[..., o:o + IDX_HEADS] * (IDX_HEADS ** -0.5)
    scale = HEAD_DIM ** -0.5
    bidx = jnp.arange(B)[:, None, None]
    key_pos = jnp.arange(T)

    def block(t0):
        tpos = t0 + jnp.arange(Q_BLOCK)
        qb = lax.dynamic_slice_in_dim(q, t0, Q_BLOCK, axis=1)
        iqb = lax.dynamic_slice_in_dim(iq, t0, Q_BLOCK, axis=1)
        iwb = lax.dynamic_slice_in_dim(iw, t0, Q_BLOCK, axis=1)
        s = jnp.einsum('bqjd,bsd->bqjs', iqb, ik) * (IDX_DIM ** -0.5)
        score = jnp.einsum('bqjs,bqj->bqs', jax.nn.relu(s), iwb).astype(jnp.float32)
        causal = key_pos[None, :] <= tpos[:, None]
        score = jnp.where(causal, score, -jnp.inf)
        _, idx = lax.top_k(score, k_sel)
        valid = idx <= tpos[None, :, None]
        ks = k[bidx, idx]
        vs = v[bidx, idx]
        logits = jnp.einsum('bqhgd,bqkhd->bqhgk', qb, ks).astype(jnp.float32) * scale
        bias = t5_bias(table, tpos[None, :, None] - idx)
        logits = logits + jnp.moveaxis(bias, 2, -1)
        logits = jnp.where(valid[:, :, None, None, :], logits, -jnp.inf)
        prob = jax.nn.softmax(logits, axis=-1).astype(v.dtype)
        out = jnp.einsum('bqhgk,bqkhd->bqhgd', prob, vs)
        return out.reshape(B, Q_BLOCK, Q_W)

    return map_query_blocks(block, T, Q_BLOCK) @ w_out


def swa_sink_mixer(h, table, w_in, b_in, sinks, w_out, b_out):
    B, T, _ = h.shape
    W = SWA_WINDOW
    nb = T // W
    q, k, v = split_qkv(h @ w_in + b_in, T)
    qb = q.reshape(B, nb, W, N_KV_HEADS, GROUP, HEAD_DIM)

    def banded(t):
        tb = t.reshape(B, nb, W, N_KV_HEADS, HEAD_DIM)
        prev = jnp.concatenate([jnp.zeros_like(tb[:, :1]), tb[:, :-1]], axis=1)
        return jnp.concatenate([prev, tb], axis=2)

    kb, vb = banded(k), banded(v)
    logits = jnp.einsum('bnqhgd,bnkhd->bnhgqk', qb, kb).astype(jnp.float32) * (HEAD_DIM ** -0.5)
    qi = jnp.arange(W)[:, None]
    kj = jnp.arange(2 * W)[None, :]
    dist = W + qi - kj
    bias = jnp.moveaxis(t5_bias(table, dist), (0, 1), (2, 3))
    kpos = jnp.arange(nb)[:, None, None] * W - W + kj[None]
    allowed = (dist >= 0) & (dist < W) & (kpos >= 0)
    logits = jnp.where(allowed[None, :, None, None], logits + bias, -jnp.inf)
    sink = jnp.broadcast_to(sinks.astype(jnp.float32).reshape(N_KV_HEADS, GROUP, 1, 1),
                            logits.shape[:-1] + (1,))
    prob = jax.nn.softmax(jnp.concatenate([logits, sink], axis=-1), axis=-1)[..., :-1]
    out = jnp.einsum('bnhgqk,bnkhd->bnqhgd', prob.astype(v.dtype), vb).reshape(B, T, Q_W)
    return out @ w_out + b_out


def moba_mixer(h, table, w_in, w_out):
    B, T, _ = h.shape
    BLK = MOBA_BLOCK
    QC = MOBA_Q_CHUNK
    q, k, v = split_qkv(h @ w_in, T)
    nblk = -(-T // BLK)
    Tp = nblk * BLK
    pad = ((0, 0), (0, Tp - T), (0, 0), (0, 0))
    kp = jnp.pad(k, pad)
    vp = jnp.pad(v, pad)
    kblk = kp.reshape(B, nblk, BLK, N_KV_HEADS, HEAD_DIM)
    vblk = vp.reshape(B, nblk, BLK, N_KV_HEADS, HEAD_DIM)
    kmean = jnp.mean(kblk.astype(jnp.float32), axis=2).astype(k.dtype)
    kbt = jnp.transpose(kblk, (0, 3, 1, 2, 4))
    vbt = jnp.transpose(vblk, (0, 3, 1, 2, 4))
    n_sel = min(MOBA_TOPK, nblk - 1)
    scale = HEAD_DIM ** -0.5
    bidx = jnp.arange(B).reshape(B, 1, 1, 1, 1)
    hidx = jnp.arange(N_KV_HEADS).reshape(1, 1, N_KV_HEADS, 1, 1)
    head_idx = jnp.arange(N_HEADS).reshape(1, 1, N_KV_HEADS, GROUP, 1, 1)
    within = jnp.arange(BLK)
    block_ids = jnp.arange(nblk)

    def chunk(t0):
        tpos = t0 + jnp.arange(QC)
        cur = t0 // BLK
        qc = lax.dynamic_slice_in_dim(q, t0, QC, axis=1)
        k_own = lax.dynamic_slice_in_dim(kp, cur * BLK, BLK, axis=1)
        v_own = lax.dynamic_slice_in_dim(vp, cur * BLK, BLK, axis=1)
        dist_own = tpos[:, None] - (cur * BLK + within)[None, :]
        l_own = jnp.einsum('bqhgd,bkhd->bqhgk', qc, k_own).astype(jnp.float32) * scale
        l_own = l_own + jnp.moveaxis(t5_bias(table, dist_own), 1, -1)
        l_own = jnp.where((dist_own >= 0)[:, None, None, :], l_own, -jnp.inf)
        if n_sel > 0:
            gate = jnp.einsum('bqhgd,bnhd->bqhgn', qc, kmean).astype(jnp.float32)
            gate = jnp.where(block_ids < cur, gate, -jnp.inf)
            _, sel = lax.top_k(gate, n_sel)
            ks = kbt[bidx, hidx, sel]
            vs = vbt[bidx, hidx, sel]
            l_sel = jnp.einsum('bqhgd,bqhgjkd->bqhgjk', qc, ks).astype(jnp.float32) * scale
            dist_sel = tpos[None, :, None, None, None, None] - (sel[..., None] * BLK + within)
            l_sel = l_sel + table[t5_bucket(dist_sel), head_idx].astype(jnp.float32)
            l_sel = jnp.where((sel < cur)[..., None], l_sel, -jnp.inf)
            l_sel = l_sel.reshape(B, QC, N_KV_HEADS, GROUP, n_sel * BLK)
            prob = jax.nn.softmax(jnp.concatenate([l_sel, l_own], axis=-1), axis=-1).astype(v.dtype)
            p_sel = prob[..., :n_sel * BLK].reshape(B, QC, N_KV_HEADS, GROUP, n_sel, BLK)
            out = (jnp.einsum('bqhgjk,bqhgjkd->bqhgd', p_sel, vs)
                   + jnp.einsum('bqhgk,bkhd->bqhgd', prob[..., n_sel * BLK:], v_own))
        else:
            prob = jax.nn.softmax(l_own, axis=-1).astype(v.dtype)
            out = jnp.einsum('bqhgk,bkhd->bqhgd', prob, v_own)
        return out.reshape(B, QC, Q_W)

    return map_query_blocks(chunk, T, MOBA_Q_CHUNK) @ w_out


def dilated_mixer(h, table, w_in, w_out):
    B, T, _ = h.shape
    q, k, v = split_qkv(h @ w_in, T)
    scale = HEAD_DIM ** -0.5

    def block(t0):
        tpos = t0 + jnp.arange(Q_BLOCK)
        qb = lax.dynamic_slice_in_dim(q, t0, Q_BLOCK, axis=1)
        outs, lses = [], []
        for window, dil in DILATED_BRANCHES:
            dist = dil * jnp.arange(window // dil + 1)
            kidx = tpos[:, None] - dist[None, :]
            valid = kidx >= 0
            kidx = jnp.maximum(kidx, 0)
            ks = k[:, kidx]
            vs = v[:, kidx]
            logits = jnp.einsum('bqhgd,bqmhd->bqhgm', qb, ks).astype(jnp.float32) * scale
            logits = logits + jnp.moveaxis(t5_bias(table, dist), 0, -1)
            logits = jnp.where(valid[:, None, None, :], logits, -jnp.inf)
            lse = jax.nn.logsumexp(logits, axis=-1, keepdims=True)
            prob = jnp.exp(logits - lse).astype(v.dtype)
            outs.append(jnp.einsum('bqhgm,bqmhd->bqhgd', prob, vs))
            lses.append(lse)
        alpha = jax.nn.softmax(jnp.concatenate(lses, axis=-1), axis=-1).astype(v.dtype)
        out = alpha[..., 0:1] * outs[0]
        for i in range(1, len(outs)):
            out = out + alpha[..., i:i + 1] * outs[i]
        return out.reshape(B, Q_BLOCK, Q_W)

    return map_query_blocks(block, T, Q_BLOCK) @ w_out


def causal_depthwise_conv(g, w, b):
    width = w.shape[0]
    T = g.shape[1]
    gp = jnp.pad(g, ((0, 0), (width - 1, 0), (0, 0)))
    y = b
    for i in range(width):
        y = y + w[i] * gp[:, i:i + T]
    return y


def conv_ffn(h, w_gate, w_up, conv_w, conv_b, w_down):
    g = causal_depthwise_conv(h @ w_gate, conv_w, conv_b)
    return (jax.nn.silu(g) * (h @ w_up)) @ w_down


def setup_inputs(seed: int = 0) -> dict:
    key = jax.random.key(seed)
    ks = jax.random.split(key, 24)
    nA, nB, nC, nD = LAYERS_PER_MIXER

    def nrm(k, shape, scale):
        return jax.random.normal(k, shape, jnp.float32) * scale

    return {
        "x": nrm(ks[0], (BATCH, SEQ, D_MODEL), 1.0),
        "rel_bias_table": nrm(ks[1], (T5_BUCKETS, N_HEADS), 0.5),
        "attn_norm": 1.0 + nrm(ks[2], (DEPTH, D_MODEL), 0.02),
        "ffn_norm": 1.0 + nrm(ks[3], (DEPTH, D_MODEL), 0.02),
        "final_norm": 1.0 + nrm(ks[4], (D_MODEL,), 0.02),
        "a_w_in": nrm(ks[5], (nA, D_MODEL, DSA_IN_W), D_MODEL ** -0.5),
        "a_w_out": nrm(ks[6], (nA, Q_W, D_MODEL), Q_W ** -0.5),
        "b_w_in": nrm(ks[7], (nB, D_MODEL, QKV_W), D_MODEL ** -0.5),
        "b_b_in": nrm(ks[8], (nB, QKV_W), 0.02),
        "b_sinks": nrm(ks[9], (nB, N_HEADS), 0.5),
        "b_w_out": nrm(ks[10], (nB, Q_W, D_MODEL), Q_W ** -0.5),
        "b_b_out": nrm(ks[11], (nB, D_MODEL), 0.02),
        "c_w_in": nrm(ks[12], (nC, D_MODEL, QKV_W), D_MODEL ** -0.5),
        "c_w_out": nrm(ks[13], (nC, Q_W, D_MODEL), Q_W ** -0.5),
        "d_w_in": nrm(ks[14], (nD, D_MODEL, QKV_W), D_MODEL ** -0.5),
        "d_w_out": nrm(ks[15], (nD, Q_W, D_MODEL), Q_W ** -0.5),
        "ffn_w_gate": nrm(ks[16], (DEPTH, D_MODEL, D_FF), D_MODEL ** -0.5),
        "ffn_w_up": nrm(ks[17], (DEPTH, D_MODEL, D_FF), D_MODEL ** -0.5),
        "ffn_conv_w": nrm(ks[18], (DEPTH, FFN_CONV_WIDTH, D_FF), FFN_CONV_WIDTH ** -0.5),
        "ffn_conv_b": nrm(ks[19], (DEPTH, D_FF), 0.02),
        "ffn_w_down": nrm(ks[20], (DEPTH, D_FF, D_MODEL), D_FF ** -0.5),
    }


def reference(x, rel_bias_table, attn_norm, ffn_norm, final_norm,
              a_w_in, a_w_out,
              b_w_in, b_b_in, b_sinks, b_w_out, b_b_out,
              c_w_in, c_w_out,
              d_w_in, d_w_out,
              ffn_w_gate, ffn_w_up, ffn_conv_w, ffn_conv_b, ffn_w_down):
    for i in range(DEPTH):
        m, j = i % N_MIXERS, i // N_MIXERS
        h = rms_norm(x, attn_norm[i])
        if m == 0:
            y = dsa_mixer(h, rel_bias_table, a_w_in[j], a_w_out[j])
        elif m == 1:
            y = swa_sink_mixer(h, rel_bias_table, b_w_in[j], b_b_in[j], b_sinks[j],
                               b_w_out[j], b_b_out[j])
        elif m == 2:
            y = moba_mixer(h, rel_bias_table, c_w_in[j], c_w_out[j])
        else:
            y = dilated_mixer(h, rel_bias_table, d_w_in[j], d_w_out[j])
        x = x + y
        h = rms_norm(x, ffn_norm[i])
        x = x + conv_ffn(h, ffn_w_gate[i], ffn_w_up[i], ffn_conv_w[i], ffn_conv_b[i], ffn_w_down[i])
    return rms_norm(x, final_norm)
```

```python
import functools
import math

import jax
import jax.numpy as jnp
from jax import lax
from jax.experimental import pallas as pl
from jax.experimental.pallas import tpu as pltpu

HEAD_DIM = 128
GROUP = 4
T5_BUCKETS = 32
T5_MAX_DIST = 2048
DSA_TOPK = 256
IDX_HEADS = 32
IDX_DIM = 128
SWA_WINDOW = 128
MOBA_BLOCK = 256
MOBA_TOPK = 3
DILATED_BRANCHES = ((128, 1), (512, 4), (2048, 16))
RMS_EPS = 1e-6

LANES = 128
NEG = -1e30
ATT_TILE = 256
ATT_CHUNK = 32
IDX_TQ = 128
IDX_TK = 256
IDX_CHUNK = 32
VMEM_LIMIT = 56 * 1024 * 1024

_NT = (((1,), (1,)), ((), ()))


def _params(*sem):
    return pltpu.CompilerParams(dimension_semantics=sem, vmem_limit_bytes=VMEM_LIMIT)


def _rmsnorm_body(x_ref, g_ref, o_ref):
    x = x_ref[...]
    ms = jnp.mean(x * x, axis=-1, keepdims=True)
    y = x * lax.rsqrt(ms + RMS_EPS)
    o_ref[...] = (y * g_ref[...]).astype(o_ref.dtype)


def _rmsnorm(x2d, gain, out_dtype):
    n, d = x2d.shape
    tm = 256
    return pl.pallas_call(
        _rmsnorm_body,
        grid=(n // tm,),
        in_specs=[pl.BlockSpec((tm, d), lambda i: (i, 0)),
                  pl.BlockSpec((1, d), lambda i: (0, 0))],
        out_specs=pl.BlockSpec((tm, d), lambda i: (i, 0)),
        out_shape=jax.ShapeDtypeStruct((n, d), out_dtype),
        compiler_params=_params("parallel"),
        name="rmsnorm",
    )(x2d, gain.reshape(1, d))


def _matmul_body(*refs, has_bias, has_res):
    a_ref, w_ref = refs[0], refs[1]
    o_ref = refs[-1]
    acc = jnp.dot(a_ref[...], w_ref[...], preferred_element_type=jnp.float32)
    k = 2
    if has_bias:
        acc = acc + refs[k][...]
        k += 1
    if has_res:
        acc = acc + refs[k][...]
    o_ref[...] = acc.astype(o_ref.dtype)


def _matmul(a, w, bias=None, res=None, *, out_dtype, tm, tn):
    n, kdim = a.shape
    wn = w.shape[1]
    assert n % tm == 0 and wn % tn == 0
    in_specs = [pl.BlockSpec((tm, kdim), lambda i, j: (i, 0)),
                pl.BlockSpec((kdim, tn), lambda i, j: (0, j))]
    args = [a, w]
    if bias is not None:
        in_specs.append(pl.BlockSpec((1, tn), lambda i, j: (0, j)))
        args.append(bias.reshape(1, wn).astype(jnp.float32))
    if res is not None:
        in_specs.append(pl.BlockSpec((tm, tn), lambda i, j: (i, j)))
        args.append(res)
    return pl.pallas_call(
        functools.partial(_matmul_body, has_bias=bias is not None, has_res=res is not None),
        grid=(n // tm, wn // tn),
        in_specs=in_specs,
        out_specs=pl.BlockSpec((tm, tn), lambda i, j: (i, j)),
        out_shape=jax.ShapeDtypeStruct((n, wn), out_dtype),
        compiler_params=_params("parallel", "arbitrary"),
        name="matmul",
    )(*args)


FFN_HALO = 16


def _ffn_body(h_ref, hp_ref, x_ref, wg_ref, wu_ref, wd_ref, cw_ref, cb_ref, o_ref, hs_ref,
              *, tm, seq):
    i = pl.program_id(0)
    f = pl.program_id(1)

    @pl.when(f == 0)
    def _():
        hs_ref[FFN_HALO:, :] = h_ref[...]
        first = (i * tm) % seq == 0
        hs_ref[:FFN_HALO, :] = jnp.where(first, jnp.zeros_like(hp_ref[...]), hp_ref[...])
        o_ref[...] = x_ref[...]

    g = jnp.dot(hs_ref[...], wg_ref[...], preferred_element_type=jnp.float32)
    u = jnp.dot(hs_ref[FFN_HALO:, :], wu_ref[...], preferred_element_type=jnp.float32)
    cw = cw_ref[...]
    y = (cb_ref[...]
         + cw[0:1, :] * g[FFN_HALO - 2:FFN_HALO - 2 + tm, :]
         + cw[1:2, :] * g[FFN_HALO - 1:FFN_HALO - 1 + tm, :]
         + cw[2:3, :] * g[FFN_HALO:, :])
    act = (y * (1.0 / (1.0 + jnp.exp(-y)))) * u
    o_ref[...] += jnp.dot(act.astype(jnp.bfloat16), wd_ref[...], preferred_element_type=jnp.float32)


def _ffn(h, x, wg, wu, wd, cw, cb, *, seq, tm=512, tf=256):
    n, d = h.shape
    fdim = wg.shape[1]
    assert n % tm == 0 and fdim % tf == 0 and seq % tm == 0 and tm % FFN_HALO == 0
    hb = tm // FFN_HALO
    return pl.pallas_call(
        functools.partial(_ffn_body, tm=tm, seq=seq),
        grid=(n // tm, fdim // tf),
        in_specs=[
            pl.BlockSpec((tm, d), lambda i, f: (i, 0), pipeline_mode=pl.Buffered(1)),
            pl.BlockSpec((FFN_HALO, d), lambda i, f: (jnp.maximum(i * hb - 1, 0), 0)),
            pl.BlockSpec((tm, d), lambda i, f: (i, 0), pipeline_mode=pl.Buffered(1)),
            pl.BlockSpec((d, tf), lambda i, f: (0, f)),
            pl.BlockSpec((d, tf), lambda i, f: (0, f)),
            pl.BlockSpec((tf, d), lambda i, f: (f, 0)),
            pl.BlockSpec((3, tf), lambda i, f: (0, f)),
            pl.BlockSpec((1, tf), lambda i, f: (0, f)),
        ],
        out_specs=pl.BlockSpec((tm, d), lambda i, f: (i, 0)),
        out_shape=jax.ShapeDtypeStruct((n, d), jnp.float32),
        scratch_shapes=[pltpu.VMEM((FFN_HALO + tm, d), jnp.bfloat16)],
        compiler_params=_params("parallel", "arbitrary"),
        name="conv_ffn",
    )(h, h, x, wg, wu, wd, cw, cb)


def _t5_bucket(dist):
    n = jnp.maximum(dist, 0)
    max_exact = T5_BUCKETS // 2
    nf = jnp.maximum(n, 1).astype(jnp.float32)
    large = max_exact + (jnp.log(nf / max_exact) / math.log(T5_MAX_DIST / max_exact)
                         * (T5_BUCKETS - max_exact)).astype(jnp.int32)
    large = jnp.minimum(large, T5_BUCKETS - 1)
    return jnp.where(n < max_exact, n, large)


def _tile_dist(n_delta):
    t = ATT_TILE
    d = jnp.arange(n_delta, dtype=jnp.int32)[:, None, None] * t
    qi = jnp.arange(t, dtype=jnp.int32)[None, :, None]
    kj = jnp.arange(t, dtype=jnp.int32)[None, None, :]
    return d + qi - kj


def _at_body(tab_ref, bkt_ref, add_ref, o_ref, *, n_heads):
    head = pl.program_id(0) * GROUP + pl.program_id(2)
    bkt = bkt_ref[...]
    val = jnp.zeros(bkt.shape, jnp.float32)
    for b in range(T5_BUCKETS):
        val = jnp.where(bkt == b, tab_ref[b * n_heads + head], val)
    o_ref[...] = val + add_ref[...]


def _additive_tiles(table, bucket, add, n_kv):
    n_delta, t, _ = bucket.shape
    n_heads = n_kv * GROUP
    out = pl.pallas_call(
        functools.partial(_at_body, n_heads=n_heads),
        grid=(n_kv, n_delta, GROUP),
        in_specs=[pl.BlockSpec(memory_space=pltpu.SMEM),
                  pl.BlockSpec((None, t, t), lambda h, d, g: (d, 0, 0)),
                  pl.BlockSpec((None, t, t), lambda h, d, g: (d, 0, 0))],
        out_specs=pl.BlockSpec((None, None, None, t, t), lambda h, d, g: (h, d, g, 0, 0)),
        out_shape=jax.ShapeDtypeStruct((n_kv, n_delta, GROUP, t, t), jnp.float32),
        compiler_params=_params("parallel", "parallel", "parallel"),
        name="bias_tiles",
    )(table.reshape(-1), bucket, add)
    return out.reshape(n_kv, n_delta, GROUP * t, t)


def _mixer_tiles(table, n_kv):
    max_exact = T5_BUCKETS // 2
    far = math.ceil(max_exact * (T5_MAX_DIST / max_exact)
                    ** ((T5_BUCKETS - 1 - max_exact) / (T5_BUCKETS - max_exact))) + 2
    nd_causal = -(-(far + ATT_TILE - 1) // ATT_TILE) + 1
    dist = _tile_dist(nd_causal)
    causal = _additive_tiles(table, _t5_bucket(dist),
                             jnp.where(dist >= 0, 0.0, NEG).astype(jnp.float32), n_kv)

    nd_swa = (SWA_WINDOW - 1 + ATT_TILE - 1) // ATT_TILE + 1
    dist = _tile_dist(nd_swa)
    swa = _additive_tiles(table, _t5_bucket(dist),
                          jnp.where((dist >= 0) & (dist < SWA_WINDOW), 0.0, NEG).astype(jnp.float32),
                          n_kv)

    max_w = max(w for w, _ in DILATED_BRANCHES)
    nd_dil = (max_w + ATT_TILE - 1) // ATT_TILE + 1
    dist = _tile_dist(nd_dil)
    count = jnp.zeros(dist.shape, jnp.int32)
    for w, r in DILATED_BRANCHES:
        count = count + ((dist >= 0) & (dist <= w) & (dist % r == 0)).astype(jnp.int32)
    add = jnp.where(count > 0, jnp.log(jnp.maximum(count, 1).astype(jnp.float32)), NEG)
    dil = _additive_tiles(table, _t5_bucket(dist), add.astype(jnp.float32), n_kv)
    return causal, swa, dil


_KEY_NEG_INF = -2139095041


def _indexer_body(iq_ref, ik_ref, iw_ref, mask_ref, iqs_ref, wb_ref, s_ref, key_ref,
                  *, k_sel, w_scale):
    tq, tk, ch = IDX_TQ, IDX_TK, IDX_CHUNK
    i = pl.program_id(1)
    n_chunks_total = key_ref.shape[0]

    for j in range(IDX_HEADS):
        iqs_ref[j * tq:(j + 1) * tq, :] = iq_ref[:, j * IDX_DIM:(j + 1) * IDX_DIM]
    w = iw_ref[...] * w_scale
    for j in range(IDX_HEADS):
        wb_ref[j * tq:(j + 1) * tq, :] = jnp.broadcast_to(w[:, j:j + 1], (tq, LANES))

    n_tiles = (i * tq + tq + tk - 1) // tk
    n_chunks = n_tiles * (tk // LANES)

    def tile(jk, carry):
        koff = pl.multiple_of(jk * tk, tk)
        s_ref[...] = lax.dot_general(iqs_ref[...], ik_ref[pl.ds(koff, tk), :], _NT,
                                     preferred_element_type=jnp.float32)

        def chunk(c, carry2):
            r0 = pl.multiple_of(c * ch, ch)
            acc = jnp.zeros((ch, tk), jnp.float32)
            for j in range(IDX_HEADS):
                sj = s_ref[pl.ds(j * tq + r0, ch), :]
                wj = wb_ref[pl.ds(j * tq + r0, ch), :]
                acc = acc + jnp.maximum(sj, 0.0) * jnp.concatenate([wj] * (tk // LANES), axis=1)
            qpos = i * tq + r0 + lax.broadcasted_iota(jnp.int32, (ch, tk), 0)
            kpos = koff + lax.broadcasted_iota(jnp.int32, (ch, tk), 1)
            acc = jnp.where(kpos <= qpos, acc, -jnp.inf)
            bits = pltpu.bitcast(acc, jnp.int32)
            key = bits ^ ((bits >> 31) & 0x7FFFFFFF)
            for part in range(tk // LANES):
                key_ref[jk * (tk // LANES) + part, pl.ds(r0, ch), :] = key[:, part * LANES:(part + 1) * LANES]
            return carry2

        lax.fori_loop(0, tq // ch, chunk, 0)
        return carry

    lax.fori_loop(0, n_tiles, tile, 0)

    def search(it, tau):
        cand = tau + jnp.left_shift(jnp.int32(1), 31 - it)

        def count(c, cnt):
            return cnt + jnp.where(key_ref[c] >= cand, 1.0, 0.0)

        cnt = lax.fori_loop(0, n_chunks, count, jnp.zeros((tq, LANES), jnp.float32))
        total = jnp.sum(cnt, axis=-1, keepdims=True)
        return jnp.where(total >= float(k_sel), cand, tau)

    tau = lax.fori_loop(0, 32, search, jnp.full((tq, LANES), -2 ** 31, jnp.int32))

    def emit(c, carry):
        key = key_ref[c]
        keep = (key >= tau) & (key > _KEY_NEG_INF)
        mask_ref[c] = jnp.where(keep, 0.0, NEG).astype(mask_ref.dtype)
        return carry

    lax.fori_loop(0, n_chunks, emit, 0)

    def fill(c, carry):
        mask_ref[c] = jnp.full((tq, LANES), NEG, mask_ref.dtype)
        return carry

    lax.fori_loop(n_chunks, n_chunks_total, fill, 0)


def _dsa_mask(p3, iw3, *, k_sel):
    bsz, seq, _ = p3.shape
    tq = IDX_TQ
    assert seq % IDX_TK == 0
    iq_block = 0
    ik_block = IDX_HEADS
    nck = seq // LANES
    w_scale = (IDX_HEADS ** -0.5) * (IDX_DIM ** -0.5)
    return pl.pallas_call(
        functools.partial(_indexer_body, k_sel=k_sel, w_scale=w_scale),
        grid=(bsz, seq // tq),
        in_specs=[
            pl.BlockSpec((None, tq, IDX_HEADS * IDX_DIM), lambda b, i: (b, i, iq_block)),
            pl.BlockSpec((None, seq, IDX_DIM), lambda b, i: (b, 0, ik_block)),
            pl.BlockSpec((None, tq, LANES), lambda b, i: (b, i, 0)),
        ],
        out_specs=pl.BlockSpec((None, nck, tq, LANES), lambda b, i: (b, 0, i, 0)),
        out_shape=jax.ShapeDtypeStruct((bsz, nck, seq, LANES), jnp.bfloat16),
        scratch_shapes=[
            pltpu.VMEM((IDX_HEADS * tq, IDX_DIM), jnp.bfloat16),
            pltpu.VMEM((IDX_HEADS * tq, LANES), jnp.float32),
            pltpu.VMEM((IDX_HEADS * tq, IDX_TK), jnp.float32),
            pltpu.VMEM((nck, tq, LANES), jnp.int32),
        ],
        compiler_params=_params("parallel", "arbitrary"),
        name="dsa_indexer",
    )(p3, p3, iw3)


def _attn_body(*refs, kind, n_back, seq):
    t, ch = ATT_TILE, ATT_CHUNK
    rows = GROUP * t
    k = 0
    if kind == "swa":
        sink_ref = refs[k]; k += 1
    q_ref, k_ref, v_ref, at_ref = refs[k:k + 4]; k += 4
    if kind == "dsa":
        mask_ref = refs[k]; k += 1
    o_ref = refs[k]; k += 1
    qs_ref, s_ref, p_ref, m_ref, l_ref, acc_ref = refs[k:k + 6]; k += 6
    if kind == "moba":
        kmean_ref, sel_ref = refs[k:k + 2]

    h = pl.program_id(1)
    i = pl.program_id(2)
    n_delta = at_ref.shape[0]
    scale = HEAD_DIM ** -0.5

    for g in range(GROUP):
        qs_ref[g * t:(g + 1) * t, :] = q_ref[:, g * HEAD_DIM:(g + 1) * HEAD_DIM]

    if kind == "swa":
        for g in range(GROUP):
            m_ref[g * t:(g + 1) * t, :] = jnp.full((t, LANES), sink_ref[h * GROUP + g], jnp.float32)
        l_ref[...] = jnp.ones_like(l_ref)
    else:
        m_ref[...] = jnp.full_like(m_ref, NEG)
        l_ref[...] = jnp.zeros_like(l_ref)
    acc_ref[...] = jnp.zeros_like(acc_ref)

    if kind == "moba":
        n_blk = seq // MOBA_BLOCK

        @pl.when(i == 0)
        def _():
            kf = k_ref[...].astype(jnp.float32).reshape(n_blk, MOBA_BLOCK, HEAD_DIM)
            kmean_ref[...] = jnp.zeros_like(kmean_ref)
            kmean_ref[:n_blk, :] = jnp.sum(kf, axis=1) * (1.0 / MOBA_BLOCK)

        km = kmean_ref[...]
        km_hi = km.astype(jnp.bfloat16)
        km_lo = (km - km_hi.astype(jnp.float32)).astype(jnp.bfloat16)
        q_all = qs_ref[...]
        gate = (lax.dot_general(q_all, km_hi, _NT, preferred_element_type=jnp.float32)
                + lax.dot_general(q_all, km_lo, _NT, preferred_element_type=jnp.float32))
        lane = lax.broadcasted_iota(jnp.int32, (rows, LANES), 1).astype(jnp.float32)
        i_f = i.astype(jnp.float32)
        gate = jnp.where(lane < i_f, gate, -jnp.inf)
        picked = lane == i_f
        for _ in range(MOBA_TOPK):
            best = jnp.max(gate, axis=-1, keepdims=True)
            first = jnp.min(jnp.where(gate == best, lane, float(LANES)), axis=-1, keepdims=True)
            hit = (lane == first) & (best > -jnp.inf)
            picked = picked | hit
            gate = jnp.where(lane == first, -jnp.inf, gate)
        sel_ref[...] = jnp.where(picked, 0.0, NEG)

    lo = 0 if n_back is None else jnp.maximum(i - n_back, 0)

    def tile(j, carry):
        koff = pl.multiple_of(j * t, t)
        s_ref[...] = lax.dot_general(qs_ref[...], k_ref[pl.ds(koff, t), :], _NT,
                                     preferred_element_type=jnp.float32)
        d = jnp.minimum(i - j, n_delta - 1)

        def chunk(c, carry2):
            r0 = pl.multiple_of(c * ch, ch)
            s = s_ref[pl.ds(r0, ch), :] * scale + at_ref[d, pl.ds(r0, ch), :]
            if kind == "dsa":
                q0 = pl.multiple_of((c % (t // ch)) * ch, ch)
                parts = [mask_ref[j * (t // LANES) + part, pl.ds(q0, ch), :].astype(jnp.float32)
                         for part in range(t // LANES)]
                s = s + jnp.concatenate(parts, axis=1)
            if kind == "moba":
                lane = lax.broadcasted_iota(jnp.int32, (ch, LANES), 1)
                col = jnp.sum(jnp.where(lane == j, sel_ref[pl.ds(r0, ch), :], 0.0),
                              axis=-1, keepdims=True)
                s = s + col
            m_prev = m_ref[pl.ds(r0, ch), :]
            m_new = jnp.maximum(m_prev, jnp.max(s, axis=-1, keepdims=True))
            alpha = jnp.exp(m_prev - m_new)
            p = jnp.exp(s - jnp.concatenate([m_new] * (t // LANES), axis=1))
            l_ref[pl.ds(r0, ch), :] = alpha * l_ref[pl.ds(r0, ch), :] + jnp.sum(p, axis=-1, keepdims=True)
            m_ref[pl.ds(r0, ch), :] = m_new
            acc_ref[pl.ds(r0, ch), :] = acc_ref[pl.ds(r0, ch), :] * alpha
            p_ref[pl.ds(r0, ch), :] = p.astype(p_ref.dtype)
            return carry2

        lax.fori_loop(0, rows // ch, chunk, 0)
        acc_ref[...] += jnp.dot(p_ref[...], v_ref[pl.ds(koff, t), :],
                                preferred_element_type=jnp.float32)
        return carry

    lax.fori_loop(lo, i + 1, tile, 0)

    for g in range(GROUP):
        out = acc_ref[g * t:(g + 1) * t, :] / l_ref[g * t:(g + 1) * t, :]
        o_ref[:, g * HEAD_DIM:(g + 1) * HEAD_DIM] = out.astype(o_ref.dtype)


def _attention(p3, tiles, *, kind, n_heads, n_kv, sinks=None, mask=None):
    bsz, seq, _ = p3.shape
    t = ATT_TILE
    assert seq % t == 0
    rows = GROUP * t
    n_delta = tiles.shape[1]
    n_back = {"swa": n_delta - 1, "dil": n_delta - 1, "dsa": None, "moba": None}[kind]
    qw = GROUP * HEAD_DIM

    in_specs, args = [], []
    if kind == "swa":
        in_specs.append(pl.BlockSpec(memory_space=pltpu.SMEM))
        args.append(sinks.astype(jnp.float32))
    in_specs += [
        pl.BlockSpec((None, t, qw), lambda b, h, i: (b, i, h)),
        pl.BlockSpec((None, seq, HEAD_DIM), lambda b, h, i: (b, 0, n_heads + h)),
        pl.BlockSpec((None, seq, HEAD_DIM), lambda b, h, i: (b, 0, n_heads + n_kv + h)),
        pl.BlockSpec((None, n_delta, rows, t), lambda b, h, i: (h, 0, 0, 0),
                     pipeline_mode=pl.Buffered(1)),
    ]
    args += [p3, p3, p3, tiles]
    if kind == "dsa":
        in_specs.append(pl.BlockSpec((None, seq // LANES, t, LANES), lambda b, h, i: (b, 0, i, 0)))
        args.append(mask)
    scratch = [
        pltpu.VMEM((rows, HEAD_DIM), jnp.bfloat16),
        pltpu.VMEM((rows, t), jnp.float32),
        pltpu.VMEM((rows, t), jnp.bfloat16),
        pltpu.VMEM((rows, LANES), jnp.float32),
        pltpu.VMEM((rows, LANES), jnp.float32),
        pltpu.VMEM((rows, HEAD_DIM), jnp.float32),
    ]
    if kind == "moba":
        assert t == MOBA_BLOCK and seq // MOBA_BLOCK <= LANES
        scratch += [pltpu.VMEM((LANES, HEAD_DIM), jnp.float32),
                    pltpu.VMEM((rows, LANES), jnp.float32)]
    return pl.pallas_call(
        functools.partial(_attn_body, kind=kind, n_back=n_back, seq=seq),
        grid=(bsz, n_kv, seq // t),
        in_specs=in_specs,
        out_specs=pl.BlockSpec((None, t, qw), lambda b, h, i: (b, i, h)),
        out_shape=jax.ShapeDtypeStruct((bsz, seq, n_heads * HEAD_DIM), jnp.bfloat16),
        scratch_shapes=scratch,
        compiler_params=_params("parallel", "parallel", "arbitrary"),
        name="attn_" + kind,
    )(*args)


def _pad_cols(w, mult):
    pad = (-w.shape[-1]) % mult
    if pad:
        w = jnp.pad(w, [(0, 0)] * (w.ndim - 1) + [(0, pad)])
    return w


def kernel(x, rel_bias_table, attn_norm, ffn_norm, final_norm, a_w_in, a_w_out, b_w_in, b_b_in, b_sinks, b_w_out, b_b_out, c_w_in, c_w_out, d_w_in, d_w_out, ffn_w_gate, ffn_w_up, ffn_conv_w, ffn_conv_b, ffn_w_down):
    bsz, seq, d_model = x.shape
    n = bsz * seq
    n_heads = d_model // HEAD_DIM
    n_kv = n_heads // GROUP
    q_w = n_heads * HEAD_DIM
    qkv_w = q_w + 2 * n_kv * HEAD_DIM
    depth = attn_norm.shape[0]
    bf = jnp.bfloat16
    tm = 1024 if n % 1024 == 0 else 512
    tn = 512
    ffn_tf = 256

    causal_t, swa_t, dil_t = _mixer_tiles(rel_bias_table.astype(jnp.float32), n_kv)

    def in_proj(h, w, bias=None):
        w = _pad_cols(w.astype(bf), tn)
        if bias is not None:
            bias = _pad_cols(bias, tn)
        return _matmul(h, w, bias, out_dtype=bf, tm=tm, tn=tn)

    x2 = x.reshape(n, d_model)
    for li in range(depth):
        m, j = li % 4, li // 4
        h = _rmsnorm(x2, attn_norm[li], bf)
        if m == 0:
            main_w = qkv_w + IDX_HEADS * IDX_DIM + IDX_DIM
            p = in_proj(h, a_w_in[j][:, :qkv_w])
            p_idx = in_proj(h, a_w_in[j][:, qkv_w:main_w])
            iw = _matmul(h, _pad_cols(a_w_in[j][:, main_w:].astype(bf), LANES),
                         out_dtype=jnp.float32, tm=tm, tn=LANES)
            p3 = p.reshape(bsz, seq, -1)
            mask = _dsa_mask(p_idx.reshape(bsz, seq, -1), iw.reshape(bsz, seq, LANES),
                             k_sel=min(DSA_TOPK, seq // 4))
            o = _attention(p3, causal_t, kind="dsa", n_heads=n_heads, n_kv=n_kv, mask=mask)
            w_out, b_out = a_w_out[j], None
        elif m == 1:
            p = in_proj(h, b_w_in[j], b_b_in[j])
            o = _attention(p.reshape(bsz, seq, -1), swa_t, kind="swa", n_heads=n_heads, n_kv=n_kv,
                           sinks=b_sinks[j])
            w_out, b_out = b_w_out[j], b_b_out[j]
        elif m == 2:
            p = in_proj(h, c_w_in[j])
            o = _attention(p.reshape(bsz, seq, -1), causal_t, kind="moba", n_heads=n_heads, n_kv=n_kv)
            w_out, b_out = c_w_out[j], None
        else:
            p = in_proj(h, d_w_in[j])
            o = _attention(p.reshape(bsz, seq, -1), dil_t, kind="dil", n_heads=n_heads, n_kv=n_kv)
            w_out, b_out = d_w_out[j], None
        x2 = _matmul(o.reshape(n, q_w), w_out.astype(bf), b_out, x2,
                     out_dtype=jnp.float32, tm=tm, tn=tn)

        h = _rmsnorm(x2, ffn_norm[li], bf)
        wg = _pad_cols(ffn_w_gate[li].astype(bf), ffn_tf)
        wu = _pad_cols(ffn_w_up[li].astype(bf), ffn_tf)
        cw = _pad_cols(ffn_conv_w[li], ffn_tf)
        cb = _pad_cols(ffn_conv_b[li].reshape(1, -1), ffn_tf)
        wd = ffn_w_down[li].astype(bf)
        pad = wg.shape[1] - wd.shape[0]
        if pad:
            wd = jnp.pad(wd, ((0, pad), (0, 0)))
        x2 = _ffn(h, x2, wg, wu, wd, cw, cb, seq=seq, tm=min(512, seq), tf=ffn_tf)

    out = _rmsnorm(x2, final_norm, jnp.float32)
    return out.reshape(bsz, seq, d_model)
```

```python
import functools
import math

import jax
import jax.numpy as jnp
from jax import lax
from jax.experimental import pallas as pl
from jax.experimental.pallas import tpu as pltpu

HEAD_DIM = 128
GROUP = 4
T5_BUCKETS = 32
T5_MAX_DIST = 2048
DSA_TOPK = 256
IDX_HEADS = 32
IDX_DIM = 128
SWA_WINDOW = 128
MOBA_BLOCK = 256
MOBA_TOPK = 3
DILATED_BRANCHES = ((128, 1), (512, 4), (2048, 16))
RMS_EPS = 1e-6

LANES = 128
NEG = -1e30
ATT_TILE = 256
ATT_CHUNK = 32
IDX_TQ = 128
IDX_TK = 256
IDX_CHUNK = 32
VMEM_LIMIT = 56 * 1024 * 1024

_NT = (((1,), (1,)), ((), ()))


def _params(*sem):
    return pltpu.CompilerParams(dimension_semantics=sem, vmem_limit_bytes=VMEM_LIMIT)


def _rmsnorm_body(x_ref, g_ref, o_ref):
    x = x_ref[...]
    ms = jnp.mean(x * x, axis=-1, keepdims=True)
    y = x * lax.rsqrt(ms + RMS_EPS)
    o_ref[...] = (y * g_ref[...]).astype(o_ref.dtype)


def _rmsnorm(x2d, gain, out_dtype):
    n, d = x2d.shape
    tm = 256
    return pl.pallas_call(
        _rmsnorm_body,
        grid=(n // tm,),
        in_specs=[pl.BlockSpec((tm, d), lambda i: (i, 0)),
                  pl.BlockSpec((1, d), lambda i: (0, 0))],
        out_specs=pl.BlockSpec((tm, d), lambda i: (i, 0)),
        out_shape=jax.ShapeDtypeStruct((n, d), out_dtype),
        compiler_params=_params("parallel"),
        name="rmsnorm",
    )(x2d, gain.reshape(1, d))


def _matmul_body(*refs, has_bias, has_res):
    a_ref, w_ref = refs[0], refs[1]
    o_ref = refs[-1]
    acc = jnp.dot(a_ref[...], w_ref[...], preferred_element_type=jnp.float32)
    k = 2
    if has_bias:
        acc = acc + refs[k][...]
        k += 1
    if has_res:
        acc = acc + refs[k][...]
    o_ref[...] = acc.astype(o_ref.dtype)


def _matmul(a, w, bias=None, res=None, *, out_dtype, tm, tn):
    n, kdim = a.shape
    wn = w.shape[1]
    assert n % tm == 0 and wn % tn == 0
    in_specs = [pl.BlockSpec((tm, kdim), lambda i, j: (i, 0)),
                pl.BlockSpec((kdim, tn), lambda i, j: (0, j))]
    args = [a, w]
    if bias is not None:
        in_specs.append(pl.BlockSpec((1, tn), lambda i, j: (0, j)))
        args.append(bias.reshape(1, wn).astype(jnp.float32))
    if res is not None:
        in_specs.append(pl.BlockSpec((tm, tn), lambda i, j: (i, j)))
        args.append(res)
    return pl.pallas_call(
        functools.partial(_matmul_body, has_bias=bias is not None, has_res=res is not None),
        grid=(n // tm, wn // tn),
        in_specs=in_specs,
        out_specs=pl.BlockSpec((tm, tn), lambda i, j: (i, j)),
        out_shape=jax.ShapeDtypeStruct((n, wn), out_dtype),
        compiler_params=_params("parallel", "arbitrary"),
        name="matmul",
    )(*args)


FFN_HALO = 16


def _ffn_body(h_ref, hp_ref, x_ref, wg_ref, wu_ref, wd_ref, cw_ref, cb_ref, o_ref, hs_ref,
              *, tm, seq):
    i = pl.program_id(0)
    f = pl.program_id(1)

    @pl.when(f == 0)
    def _():
        hs_ref[FFN_HALO:, :] = h_ref[...]
        first = (i * tm) % seq == 0
        hs_ref[:FFN_HALO, :] = jnp.where(first, jnp.zeros_like(hp_ref[...]), hp_ref[...])
        o_ref[...] = x_ref[...]

    g = jnp.dot(hs_ref[...], wg_ref[...], preferred_element_type=jnp.float32)
    u = jnp.dot(hs_ref[FFN_HALO:, :], wu_ref[...], preferred_element_type=jnp.float32)
    cw = cw_ref[...]
    y = (cb_ref[...]
         + cw[0:1, :] * g[FFN_HALO - 2:FFN_HALO - 2 + tm, :]
         + cw[1:2, :] * g[FFN_HALO - 1:FFN_HALO - 1 + tm, :]
         + cw[2:3, :] * g[FFN_HALO:, :])
    act = (y * (1.0 / (1.0 + jnp.exp(-y)))) * u
    o_ref[...] += jnp.dot(act.astype(jnp.bfloat16), wd_ref[...], preferred_element_type=jnp.float32)


def _ffn(h, x, wg, wu, wd, cw, cb, *, seq, tm=512, tf=256):
    n, d = h.shape
    fdim = wg.shape[1]
    assert n % tm == 0 and fdim % tf == 0 and seq % tm == 0 and tm % FFN_HALO == 0
    hb = tm // FFN_HALO
    return pl.pallas_call(
        functools.partial(_ffn_body, tm=tm, seq=seq),
        grid=(n // tm, fdim // tf),
        in_specs=[
            pl.BlockSpec((tm, d), lambda i, f: (i, 0), pipeline_mode=pl.Buffered(1)),
            pl.BlockSpec((FFN_HALO, d), lambda i, f: (jnp.maximum(i * hb - 1, 0), 0)),
            pl.BlockSpec((tm, d), lambda i, f: (i, 0), pipeline_mode=pl.Buffered(1)),
            pl.BlockSpec((d, tf), lambda i, f: (0, f)),
            pl.BlockSpec((d, tf), lambda i, f: (0, f)),
            pl.BlockSpec((tf, d), lambda i, f: (f, 0)),
            pl.BlockSpec((3, tf), lambda i, f: (0, f)),
            pl.BlockSpec((1, tf), lambda i, f: (0, f)),
        ],
        out_specs=pl.BlockSpec((tm, d), lambda i, f: (i, 0)),
        out_shape=jax.ShapeDtypeStruct((n, d), jnp.float32),
        scratch_shapes=[pltpu.VMEM((FFN_HALO + tm, d), jnp.bfloat16)],
        compiler_params=_params("parallel", "arbitrary"),
        name="conv_ffn",
    )(h, h, x, wg, wu, wd, cw, cb)


def _t5_bucket(dist):
    n = jnp.maximum(dist, 0)
    max_exact = T5_BUCKETS // 2
    nf = jnp.maximum(n, 1).astype(jnp.float32)
    large = max_exact + (jnp.log(nf / max_exact) / math.log(T5_MAX_DIST / max_exact)
                         * (T5_BUCKETS - max_exact)).astype(jnp.int32)
    large = jnp.minimum(large, T5_BUCKETS - 1)
    return jnp.where(n < max_exact, n, large)


def _tile_dist(n_delta):
    t = ATT_TILE
    d = jnp.arange(n_delta, dtype=jnp.int32)[:, None, None] * t
    qi = jnp.arange(t, dtype=jnp.int32)[None, :, None]
    kj = jnp.arange(t, dtype=jnp.int32)[None, None, :]
    return d + qi - kj


def _at_body(tab_ref, bkt_ref, add_ref, o_ref, *, n_heads):
    head = pl.program_id(0) * GROUP + pl.program_id(2)
    bkt = bkt_ref[...]
    val = jnp.zeros(bkt.shape, jnp.float32)
    for b in range(T5_BUCKETS):
        val = jnp.where(bkt == b, tab_ref[b * n_heads + head], val)
    o_ref[...] = val + add_ref[...]


def _additive_tiles(table, bucket, add, n_kv):
    n_delta, t, _ = bucket.shape
    n_heads = n_kv * GROUP
    out = pl.pallas_call(
        functools.partial(_at_body, n_heads=n_heads),
        grid=(n_kv, n_delta, GROUP),
        in_specs=[pl.BlockSpec(memory_space=pltpu.SMEM),
                  pl.BlockSpec((None, t, t), lambda h, d, g: (d, 0, 0)),
                  pl.BlockSpec((None, t, t), lambda h, d, g: (d, 0, 0))],
        out_specs=pl.BlockSpec((None, None, None, t, t), lambda h, d, g: (h, d, g, 0, 0)),
        out_shape=jax.ShapeDtypeStruct((n_kv, n_delta, GROUP, t, t), jnp.float32),
        compiler_params=_params("parallel", "parallel", "parallel"),
        name="bias_tiles",
    )(table.reshape(-1), bucket, add)
    return out.reshape(n_kv, n_delta, GROUP * t, t)


def _mixer_tiles(table, n_kv):
    max_exact = T5_BUCKETS // 2
    far = math.ceil(max_exact * (T5_MAX_DIST / max_exact)
                    ** ((T5_BUCKETS - 1 - max_exact) / (T5_BUCKETS - max_exact))) + 2
    nd_causal = -(-(far + ATT_TILE - 1) // ATT_TILE) + 1
    dist = _tile_dist(nd_causal)
    causal = _additive_tiles(table, _t5_bucket(dist),
                             jnp.where(dist >= 0, 0.0, NEG).astype(jnp.float32), n_kv)

    nd_swa = (SWA_WINDOW - 1 + ATT_TILE - 1) // ATT_TILE + 1
    dist = _tile_dist(nd_swa)
    swa = _additive_tiles(table, _t5_bucket(dist),
                          jnp.where((dist >= 0) & (dist < SWA_WINDOW), 0.0, NEG).astype(jnp.float32),
                          n_kv)

    max_w = max(w for w, _ in DILATED_BRANCHES)
    nd_dil = (max_w + ATT_TILE - 1) // ATT_TILE + 1
    dist = _tile_dist(nd_dil)
    count = jnp.zeros(dist.shape, jnp.int32)
    for w, r in DILATED_BRANCHES:
        count = count + ((dist >= 0) & (dist <= w) & (dist % r == 0)).astype(jnp.int32)
    add = jnp.where(count > 0, jnp.log(jnp.maximum(count, 1).astype(jnp.float32)), NEG)
    dil = _additive_tiles(table, _t5_bucket(dist), add.astype(jnp.float32), n_kv)
    return causal, swa, dil


_KEY_NEG_INF = -2139095041


def _indexer_body(iq_ref, ik_ref, iw_ref, mask_ref, iqs_ref, wb_ref, s_ref, key_ref,
                  *, k_sel, w_scale):
    tq, tk, ch = IDX_TQ, IDX_TK, IDX_CHUNK
    i = pl.program_id(1)
    n_chunks_total = key_ref.shape[0]

    for j in range(IDX_HEADS):
        iqs_ref[j * tq:(j + 1) * tq, :] = iq_ref[:, j * IDX_DIM:(j + 1) * IDX_DIM]
    w = iw_ref[...] * w_scale
    for j in range(IDX_HEADS):
        wb_ref[j * tq:(j + 1) * tq, :] = jnp.broadcast_to(w[:, j:j + 1], (tq, LANES))

    n_tiles = (i * tq + tq + tk - 1) // tk
    n_chunks = n_tiles * (tk // LANES)

    def tile(jk, carry):
        koff = pl.multiple_of(jk * tk, tk)
        s_ref[...] = lax.dot_general(iqs_ref[...], ik_ref[pl.ds(koff, tk), :], _NT,
                                     preferred_element_type=jnp.float32)

        def chunk(c, carry2):
            r0 = pl.multiple_of(c * ch, ch)
            acc = jnp.zeros((ch, tk), jnp.float32)
            for j in range(IDX_HEADS):
                sj = s_ref[pl.ds(j * tq + r0, ch), :]
                wj = wb_ref[pl.ds(j * tq + r0, ch), :]
                acc = acc + jnp.maximum(sj, 0.0) * jnp.concatenate([wj] * (tk // LANES), axis=1)
            qpos = i * tq + r0 + lax.broadcasted_iota(jnp.int32, (ch, tk), 0)
            kpos = koff + lax.broadcasted_iota(jnp.int32, (ch, tk), 1)
            acc = jnp.where(kpos <= qpos, acc, -jnp.inf)
            bits = pltpu.bitcast(acc, jnp.int32)
            key = bits ^ ((bits >> 31) & 0x7FFFFFFF)
            for part in range(tk // LANES):
                key_ref[jk * (tk // LANES) + part, pl.ds(r0, ch), :] = key[:, part * LANES:(part + 1) * LANES]
            return carry2

        lax.fori_loop(0, tq // ch, chunk, 0)
        return carry

    lax.fori_loop(0, n_tiles, tile, 0)

    def search(it, tau):
        cand = tau + jnp.left_shift(jnp.int32(1), 31 - it)

        def count(c, cnt):
            return cnt + jnp.where(key_ref[c] >= cand, 1.0, 0.0)

        cnt = lax.fori_loop(0, n_chunks, count, jnp.zeros((tq, LANES), jnp.float32))
        total = jnp.sum(cnt, axis=-1, keepdims=True)
        return jnp.where(total >= float(k_sel), cand, tau)

    tau = lax.fori_loop(0, 32, search, jnp.full((tq, LANES), -2 ** 31, jnp.int32))

    def emit(c, carry):
        key = key_ref[c]
        keep = (key >= tau) & (key > _KEY_NEG_INF)
        mask_ref[c] = jnp.where(keep, 0.0, NEG).astype(mask_ref.dtype)
        return carry

    lax.fori_loop(0, n_chunks, emit, 0)

    def fill(c, carry):
        mask_ref[c] = jnp.full((tq, LANES), NEG, mask_ref.dtype)
        return carry

    lax.fori_loop(n_chunks, n_chunks_total, fill, 0)


def _dsa_mask(p3, iw3, *, k_sel):
    bsz, seq, _ = p3.shape
    tq = IDX_TQ
    assert seq % IDX_TK == 0
    iq_block = 0
    ik_block = IDX_HEADS
    nck = seq // LANES
    w_scale = (IDX_HEADS ** -0.5) * (IDX_DIM ** -0.5)
    return pl.pallas_call(
        functools.partial(_indexer_body, k_sel=k_sel, w_scale=w_scale),
        grid=(bsz, seq // tq),
        in_specs=[
            pl.BlockSpec((None, tq, IDX_HEADS * IDX_DIM), lambda b, i: (b, i, iq_block)),
            pl.BlockSpec((None, seq, IDX_DIM), lambda b, i: (b, 0, ik_block)),
            pl.BlockSpec((None, tq, LANES), lambda b, i: (b, i, 0)),
        ],
        out_specs=pl.BlockSpec((None, nck, tq, LANES), lambda b, i: (b, 0, i, 0)),
        out_shape=jax.ShapeDtypeStruct((bsz, nck, seq, LANES), jnp.bfloat16),
        scratch_shapes=[
            pltpu.VMEM((IDX_HEADS * tq, IDX_DIM), jnp.bfloat16),
            pltpu.VMEM((IDX_HEADS * tq, LANES), jnp.float32),
            pltpu.VMEM((IDX_HEADS * tq, IDX_TK), jnp.float32),
            pltpu.VMEM((nck, tq, LANES), jnp.int32),
        ],
        compiler_params=_params("parallel", "arbitrary"),
        name="dsa_indexer",
    )(p3, p3, iw3)


def _attn_body(*refs, kind, n_back, seq):
    t = ATT_TILE
    rows = GROUP * t
    k = 0
    if kind == "swa":
        sink_ref = refs[k]; k += 1
    q_ref, k_ref, v_ref, at_ref = refs[k:k + 4]; k += 4
    if kind == "dsa":
        mask_ref = refs[k]; k += 1
    o_ref = refs[k]; k += 1
    qs_ref, m_ref, acc_ref = refs[k:k + 3]; k += 3
    if kind == "moba":
        kmean_ref = refs[k]

    h = pl.program_id(1)
    i = pl.program_id(2)
    n_delta = at_ref.shape[0]
    scale = HEAD_DIM ** -0.5

    for g in range(GROUP):
        qs_ref[g * t:(g + 1) * t, :HEAD_DIM] = q_ref[:, g * HEAD_DIM:(g + 1) * HEAD_DIM]

    if kind == "swa":
        for g in range(GROUP):
            m_ref[g * t:(g + 1) * t, :] = jnp.full((t, LANES), sink_ref[h * GROUP + g], jnp.float32)
        acc_ref[:, :HEAD_DIM] = jnp.zeros((rows, HEAD_DIM), jnp.float32)
        acc_ref[:, HEAD_DIM:] = jnp.ones((rows, LANES), jnp.float32)
    else:
        m_ref[...] = jnp.full_like(m_ref, NEG)
        acc_ref[...] = jnp.zeros_like(acc_ref)

    if kind == "moba":
        n_blk = seq // MOBA_BLOCK

        @pl.when(i == 0)
        def _():
            kf = k_ref[...].astype(jnp.float32).reshape(n_blk, MOBA_BLOCK, HEAD_DIM)
            kmean_ref[...] = jnp.zeros_like(kmean_ref)
            kmean_ref[:n_blk, :] = jnp.sum(kf, axis=1) * (1.0 / MOBA_BLOCK)

        km = kmean_ref[...]
        km_hi = km.astype(jnp.bfloat16)
        km_lo = (km - km_hi.astype(jnp.float32)).astype(jnp.bfloat16)
        q_all = qs_ref[:, :HEAD_DIM]
        gate = (lax.dot_general(q_all, km_hi, _NT, preferred_element_type=jnp.float32)
                + lax.dot_general(q_all, km_lo, _NT, preferred_element_type=jnp.float32))
        lane = lax.broadcasted_iota(jnp.int32, (rows, LANES), 1).astype(jnp.float32)
        i_f = i.astype(jnp.float32)
        gate = jnp.where(lane < i_f, gate, -jnp.inf)
        picked = lane == i_f
        for _ in range(MOBA_TOPK):
            best = jnp.max(gate, axis=-1, keepdims=True)
            first = jnp.min(jnp.where(gate == best, lane, float(LANES)), axis=-1, keepdims=True)
            hit = (lane == first) & (best > -jnp.inf)
            picked = picked | hit
            gate = jnp.where(lane == first, -jnp.inf, gate)
        qs_ref[:, HEAD_DIM:] = jnp.where(picked, 0.0, NEG / scale).astype(jnp.bfloat16)

    lo = 0 if n_back is None else jnp.maximum(i - n_back, 0)
    ones = jnp.ones((t, HEAD_DIM), jnp.bfloat16)

    def tile(j, carry):
        koff = pl.multiple_of(j * t, t)
        kt = k_ref[pl.ds(koff, t), :]
        if kind == "moba":
            blk = lax.broadcasted_iota(jnp.int32, (t, LANES), 1)
            kt = jnp.concatenate([kt, jnp.where(blk == j, 1.0, 0.0).astype(jnp.bfloat16)], axis=1)
        vt = jnp.concatenate([v_ref[pl.ds(koff, t), :], ones], axis=1)
        d = jnp.minimum(i - j, n_delta - 1)
        if kind == "dsa":
            sel = jnp.concatenate([mask_ref[j * (t // LANES) + part] for part in range(t // LANES)],
                                  axis=1).astype(jnp.float32)
        for g in range(GROUP):
            r = slice(g * t, (g + 1) * t)
            s = lax.dot_general(qs_ref[r, :], kt, _NT, preferred_element_type=jnp.float32)
            s = s * scale + at_ref[d, r, :]
            if kind == "dsa":
                s = s + sel
            m_prev = m_ref[r, :]
            m_new = jnp.maximum(m_prev, jnp.max(s, axis=-1, keepdims=True))
            alpha = jnp.exp(m_prev - m_new)
            p = jnp.exp(s - jnp.concatenate([m_new] * (t // LANES), axis=1))
            m_ref[r, :] = m_new
            acc_ref[r, :] = (acc_ref[r, :] * jnp.concatenate([alpha, alpha], axis=1)
                             + jnp.dot(p.astype(jnp.bfloat16), vt, preferred_element_type=jnp.float32))
        return carry

    lax.fori_loop(lo, i + 1, tile, 0)

    for g in range(GROUP):
        r = slice(g * t, (g + 1) * t)
        out = acc_ref[r, :HEAD_DIM] / acc_ref[r, HEAD_DIM:]
        o_ref[:, g * HEAD_DIM:(g + 1) * HEAD_DIM] = out.astype(o_ref.dtype)


def _attention(p3, tiles, *, kind, n_heads, n_kv, sinks=None, mask=None):
    bsz, seq, _ = p3.shape
    t = ATT_TILE
    assert seq % t == 0
    rows = GROUP * t
    n_delta = tiles.shape[1]
    n_back = {"swa": n_delta - 1, "dil": n_delta - 1, "dsa": None, "moba": None}[kind]
    qw = GROUP * HEAD_DIM

    in_specs, args = [], []
    if kind == "swa":
        in_specs.append(pl.BlockSpec(memory_space=pltpu.SMEM))
        args.append(sinks.astype(jnp.float32))
    in_specs += [
        pl.BlockSpec((None, t, qw), lambda b, h, i: (b, i, h)),
        pl.BlockSpec((None, seq, HEAD_DIM), lambda b, h, i: (b, 0, n_heads + h)),
        pl.BlockSpec((None, seq, HEAD_DIM), lambda b, h, i: (b, 0, n_heads + n_kv + h)),
        pl.BlockSpec((None, n_delta, rows, t), lambda b, h, i: (h, 0, 0, 0),
                     pipeline_mode=pl.Buffered(1)),
    ]
    args += [p3, p3, p3, tiles]
    if kind == "dsa":
        in_specs.append(pl.BlockSpec((None, seq // LANES, t, LANES), lambda b, h, i: (b, 0, i, 0)))
        args.append(mask)
    q_cols = HEAD_DIM + (LANES if kind == "moba" else 0)
    scratch = [
        pltpu.VMEM((rows, q_cols), jnp.bfloat16),
        pltpu.VMEM((rows, LANES), jnp.float32),
        pltpu.VMEM((rows, HEAD_DIM + LANES), jnp.float32),
    ]
    if kind == "moba":
        assert t == MOBA_BLOCK and seq // MOBA_BLOCK <= LANES
        scratch += [pltpu.VMEM((LANES, HEAD_DIM), jnp.float32)]
    return pl.pallas_call(
        functools.partial(_attn_body, kind=kind, n_back=n_back, seq=seq),
        grid=(bsz, n_kv, seq // t),
        in_specs=in_specs,
        out_specs=pl.BlockSpec((None, t, qw), lambda b, h, i: (b, i, h)),
        out_shape=jax.ShapeDtypeStruct((bsz, seq, n_heads * HEAD_DIM), jnp.bfloat16),
        scratch_shapes=scratch,
        compiler_params=_params("parallel", "parallel", "arbitrary"),
        name="attn_" + kind,
    )(*args)


def _pad_cols(w, mult):
    pad = (-w.shape[-1]) % mult
    if pad:
        w = jnp.pad(w, [(0, 0)] * (w.ndim - 1) + [(0, pad)])
    return w


def kernel(x, rel_bias_table, attn_norm, ffn_norm, final_norm, a_w_in, a_w_out, b_w_in, b_b_in, b_sinks, b_w_out, b_b_out, c_w_in, c_w_out, d_w_in, d_w_out, ffn_w_gate, ffn_w_up, ffn_conv_w, ffn_conv_b, ffn_w_down):
    bsz, seq, d_model = x.shape
    n = bsz * seq
    n_heads = d_model // HEAD_DIM
    n_kv = n_heads // GROUP
    q_w = n_heads * HEAD_DIM
    qkv_w = q_w + 2 * n_kv * HEAD_DIM
    depth = attn_norm.shape[0]
    bf = jnp.bfloat16
    tm = 1024 if n % 1024 == 0 else 512
    tn = 512
    ffn_tf = 256

    causal_t, swa_t, dil_t = _mixer_tiles(rel_bias_table.astype(jnp.float32), n_kv)

    def in_proj(h, w, bias=None):
        w = _pad_cols(w.astype(bf), tn)
        if bias is not None:
            bias = _pad_cols(bias, tn)
        return _matmul(h, w, bias, out_dtype=bf, tm=tm, tn=tn)

    x2 = x.reshape(n, d_model)
    for li in range(depth):
        m, j = li % 4, li // 4
        h = _rmsnorm(x2, attn_norm[li], bf)
        if m == 0:
            main_w = qkv_w + IDX_HEADS * IDX_DIM + IDX_DIM
            p = in_proj(h, a_w_in[j][:, :qkv_w])
            p_idx = in_proj(h, a_w_in[j][:, qkv_w:main_w])
            iw = _matmul(h, _pad_cols(a_w_in[j][:, main_w:].astype(bf), LANES),
                         out_dtype=jnp.float32, tm=tm, tn=LANES)
            p3 = p.reshape(bsz, seq, -1)
            mask = _dsa_mask(p_idx.reshape(bsz, seq, -1), iw.reshape(bsz, seq, LANES),
                             k_sel=min(DSA_TOPK, seq // 4))
            o = _attention(p3, causal_t, kind="dsa", n_heads=n_heads, n_kv=n_kv, mask=mask)
            w_out, b_out = a_w_out[j], None
        elif m == 1:
            p = in_proj(h, b_w_in[j], b_b_in[j])
            o = _attention(p.reshape(bsz, seq, -1), swa_t, kind="swa", n_heads=n_heads, n_kv=n_kv,
                           sinks=b_sinks[j])
            w_out, b_out = b_w_out[j], b_b_out[j]
        elif m == 2:
            p = in_proj(h, c_w_in[j])
            o = _attention(p.reshape(bsz, seq, -1), causal_t, kind="moba", n_heads=n_heads, n_kv=n_kv)
            w_out, b_out = c_w_out[j], None
        else:
            p = in_proj(h, d_w_in[j])
            o = _attention(p.reshape(bsz, seq, -1), dil_t, kind="dil", n_heads=n_heads, n_kv=n_kv)
            w_out, b_out = d_w_out[j], None
        x2 = _matmul(o.reshape(n, q_w), w_out.astype(bf), b_out, x2,
                     out_dtype=jnp.float32, tm=tm, tn=tn)

        h = _rmsnorm(x2, ffn_norm[li], bf)
        wg = _pad_cols(ffn_w_gate[li].astype(bf), ffn_tf)
        wu = _pad_cols(ffn_w_up[li].astype(bf), ffn_tf)
        cw = _pad_cols(ffn_conv_w[li], ffn_tf)
        cb = _pad_cols(ffn_conv_b[li].reshape(1, -1), ffn_tf)
        wd = ffn_w_down[li].astype(bf)
        pad = wg.shape[1] - wd.shape[0]
        if pad:
            wd = jnp.pad(wd, ((0, pad), (0, 0)))
        x2 = _ffn(h, x2, wg, wu, wd, cw, cb, seq=seq, tm=min(512, seq), tf=ffn_tf)

    out = _rmsnorm(x2, final_norm, jnp.float32)
    return out.reshape(bsz, seq, d_model)
```

```python
import functools
import math

import jax
import jax.numpy as jnp
from jax import lax
from jax.experimental import pallas as pl
from jax.experimental.pallas import tpu as pltpu

HEAD_DIM = 128
GROUP = 4
T5_BUCKETS = 32
T5_MAX_DIST = 2048
DSA_TOPK = 256
IDX_HEADS = 32
IDX_DIM = 128
SWA_WINDOW = 128
MOBA_BLOCK = 256
MOBA_TOPK = 3
DILATED_BRANCHES = ((128, 1), (512, 4), (2048, 16))
RMS_EPS = 1e-6

LANES = 128
NEG = -1e30
LOG2E = math.log2(math.e)
ATT_TILE = 256
IDX_TQ = 128
IDX_TK = 256
IDX_PANEL_HEADS = 4
VMEM_LIMIT = 56 * 1024 * 1024

_NT = (((1,), (1,)), ((), ()))


def _params(*sem):
    return pltpu.CompilerParams(dimension_semantics=sem, vmem_limit_bytes=VMEM_LIMIT)


def _rmsnorm_body(x_ref, g_ref, o_ref):
    x = x_ref[...]
    ms = jnp.mean(x * x, axis=-1, keepdims=True)
    y = x * lax.rsqrt(ms + RMS_EPS)
    o_ref[...] = (y * g_ref[...]).astype(o_ref.dtype)


def _rmsnorm(x2d, gain, out_dtype):
    n, d = x2d.shape
    tm = 256
    return pl.pallas_call(
        _rmsnorm_body,
        grid=(n // tm,),
        in_specs=[pl.BlockSpec((tm, d), lambda i: (i, 0)),
                  pl.BlockSpec((1, d), lambda i: (0, 0))],
        out_specs=pl.BlockSpec((tm, d), lambda i: (i, 0)),
        out_shape=jax.ShapeDtypeStruct((n, d), out_dtype),
        compiler_params=_params("parallel"),
        name="rmsnorm",
    )(x2d, gain.reshape(1, d))


def _matmul_body(*refs, has_bias, has_res):
    a_ref, w_ref = refs[0], refs[1]
    o_ref = refs[-1]
    acc = jnp.dot(a_ref[...], w_ref[...], preferred_element_type=jnp.float32)
    k = 2
    if has_bias:
        acc = acc + refs[k][...]
        k += 1
    if has_res:
        acc = acc + refs[k][...]
    o_ref[...] = acc.astype(o_ref.dtype)


def _matmul(a, w, bias=None, res=None, *, out_dtype, tm, tn):
    n, kdim = a.shape
    wn = w.shape[1]
    assert n % tm == 0 and wn % tn == 0
    in_specs = [pl.BlockSpec((tm, kdim), lambda i, j: (i, 0)),
                pl.BlockSpec((kdim, tn), lambda i, j: (0, j))]
    args = [a, w]
    if bias is not None:
        in_specs.append(pl.BlockSpec((1, tn), lambda i, j: (0, j)))
        args.append(bias.reshape(1, wn).astype(jnp.float32))
    if res is not None:
        in_specs.append(pl.BlockSpec((tm, tn), lambda i, j: (i, j)))
        args.append(res)
    return pl.pallas_call(
        functools.partial(_matmul_body, has_bias=bias is not None, has_res=res is not None),
        grid=(n // tm, wn // tn),
        in_specs=in_specs,
        out_specs=pl.BlockSpec((tm, tn), lambda i, j: (i, j)),
        out_shape=jax.ShapeDtypeStruct((n, wn), out_dtype),
        compiler_params=_params("parallel", "arbitrary"),
        name="matmul",
    )(*args)


FFN_HALO = 16


def _ffn_body(h_ref, hp_ref, x_ref, wg_ref, wu_ref, wd_ref, cw_ref, cb_ref, o_ref, hs_ref,
              *, tm, seq):
    i = pl.program_id(0)
    f = pl.program_id(1)

    @pl.when(f == 0)
    def _():
        hs_ref[FFN_HALO:, :] = h_ref[...]
        first = (i * tm) % seq == 0
        hs_ref[:FFN_HALO, :] = jnp.where(first, jnp.zeros_like(hp_ref[...]), hp_ref[...])
        o_ref[...] = x_ref[...]

    g = jnp.dot(hs_ref[...], wg_ref[...], preferred_element_type=jnp.float32)
    u = jnp.dot(hs_ref[FFN_HALO:, :], wu_ref[...], preferred_element_type=jnp.float32)
    cw = cw_ref[...]
    y = (cb_ref[...]
         + cw[0:1, :] * g[FFN_HALO - 2:FFN_HALO - 2 + tm, :]
         + cw[1:2, :] * g[FFN_HALO - 1:FFN_HALO - 1 + tm, :]
         + cw[2:3, :] * g[FFN_HALO:, :])
    act = (y * (1.0 / (1.0 + jnp.exp(-y)))) * u
    o_ref[...] += jnp.dot(act.astype(jnp.bfloat16), wd_ref[...], preferred_element_type=jnp.float32)


def _ffn(h, x, wg, wu, wd, cw, cb, *, seq, tm=512, tf=256):
    n, d = h.shape
    fdim = wg.shape[1]
    assert n % tm == 0 and fdim % tf == 0 and seq % tm == 0 and tm % FFN_HALO == 0
    hb = tm // FFN_HALO
    return pl.pallas_call(
        functools.partial(_ffn_body, tm=tm, seq=seq),
        grid=(n // tm, fdim // tf),
        in_specs=[
            pl.BlockSpec((tm, d), lambda i, f: (i, 0), pipeline_mode=pl.Buffered(1)),
            pl.BlockSpec((FFN_HALO, d), lambda i, f: (jnp.maximum(i * hb - 1, 0), 0)),
            pl.BlockSpec((tm, d), lambda i, f: (i, 0), pipeline_mode=pl.Buffered(1)),
            pl.BlockSpec((d, tf), lambda i, f: (0, f)),
            pl.BlockSpec((d, tf), lambda i, f: (0, f)),
            pl.BlockSpec((tf, d), lambda i, f: (f, 0)),
            pl.BlockSpec((3, tf), lambda i, f: (0, f)),
            pl.BlockSpec((1, tf), lambda i, f: (0, f)),
        ],
        out_specs=pl.BlockSpec((tm, d), lambda i, f: (i, 0)),
        out_shape=jax.ShapeDtypeStruct((n, d), jnp.float32),
        scratch_shapes=[pltpu.VMEM((FFN_HALO + tm, d), jnp.bfloat16)],
        compiler_params=_params("parallel", "arbitrary"),
        name="conv_ffn",
    )(h, h, x, wg, wu, wd, cw, cb)


def _t5_bucket(dist):
    n = jnp.maximum(dist, 0)
    max_exact = T5_BUCKETS // 2
    nf = jnp.maximum(n, 1).astype(jnp.float32)
    large = max_exact + (jnp.log(nf / max_exact) / math.log(T5_MAX_DIST / max_exact)
                         * (T5_BUCKETS - max_exact)).astype(jnp.int32)
    large = jnp.minimum(large, T5_BUCKETS - 1)
    return jnp.where(n < max_exact, n, large)


def _tile_dist(n_delta):
    t = ATT_TILE
    d = jnp.arange(n_delta, dtype=jnp.int32)[:, None, None] * t
    qi = jnp.arange(t, dtype=jnp.int32)[None, :, None]
    kj = jnp.arange(t, dtype=jnp.int32)[None, None, :]
    return d + qi - kj


def _at_body(tab_ref, bkt_ref, add_ref, o_ref, *, n_heads):
    head = pl.program_id(0) * GROUP + pl.program_id(2)
    bkt = bkt_ref[...]
    val = jnp.zeros(bkt.shape, jnp.float32)
    for b in range(T5_BUCKETS):
        val = jnp.where(bkt == b, tab_ref[b * n_heads + head], val)
    o_ref[...] = (val + add_ref[...]) * LOG2E


def _additive_tiles(table, bucket, add, n_kv):
    n_delta, t, _ = bucket.shape
    n_heads = n_kv * GROUP
    out = pl.pallas_call(
        functools.partial(_at_body, n_heads=n_heads),
        grid=(n_kv, n_delta, GROUP),
        in_specs=[pl.BlockSpec(memory_space=pltpu.SMEM),
                  pl.BlockSpec((None, t, t), lambda h, d, g: (d, 0, 0)),
                  pl.BlockSpec((None, t, t), lambda h, d, g: (d, 0, 0))],
        out_specs=pl.BlockSpec((None, None, None, t, t), lambda h, d, g: (h, d, g, 0, 0)),
        out_shape=jax.ShapeDtypeStruct((n_kv, n_delta, GROUP, t, t), jnp.float32),
        compiler_params=_params("parallel", "parallel", "parallel"),
        name="bias_tiles",
    )(table.reshape(-1), bucket, add)
    return out.reshape(n_kv, n_delta, GROUP * t, t)


def _mixer_tiles(table, n_kv):
    max_exact = T5_BUCKETS // 2
    far = math.ceil(max_exact * (T5_MAX_DIST / max_exact)
                    ** ((T5_BUCKETS - 1 - max_exact) / (T5_BUCKETS - max_exact))) + 2
    nd_causal = -(-(far + ATT_TILE - 1) // ATT_TILE) + 1
    dist = _tile_dist(nd_causal)
    causal = _additive_tiles(table, _t5_bucket(dist),
                             jnp.where(dist >= 0, 0.0, NEG).astype(jnp.float32), n_kv)

    nd_swa = (SWA_WINDOW - 1 + ATT_TILE - 1) // ATT_TILE + 1
    dist = _tile_dist(nd_swa)
    swa = _additive_tiles(table, _t5_bucket(dist),
                          jnp.where((dist >= 0) & (dist < SWA_WINDOW), 0.0, NEG).astype(jnp.float32),
                          n_kv)

    max_w = max(w for w, _ in DILATED_BRANCHES)
    nd_dil = (max_w + ATT_TILE - 1) // ATT_TILE + 1
    dist = _tile_dist(nd_dil)
    count = jnp.zeros(dist.shape, jnp.int32)
    for w, r in DILATED_BRANCHES:
        count = count + ((dist >= 0) & (dist <= w) & (dist % r == 0)).astype(jnp.int32)
    add = jnp.where(count > 0, jnp.log(jnp.maximum(count, 1).astype(jnp.float32)), NEG)
    dil = _additive_tiles(table, _t5_bucket(dist), add.astype(jnp.float32), n_kv)
    return causal, swa, dil


_KEY_NEG_INF = -2139095041


def _indexer_body(iq_ref, ik_ref, iw_ref, mask_ref, iqs_ref, wb_ref, key_ref, *, k_sel, w_scale):
    tq, tk = IDX_TQ, IDX_TK
    i = pl.program_id(1)
    n_chunks_total = key_ref.shape[0]

    for j in range(IDX_HEADS):
        iqs_ref[j * tq:(j + 1) * tq, :] = iq_ref[:, j * IDX_DIM:(j + 1) * IDX_DIM]
    w = iw_ref[...] * w_scale
    for j in range(IDX_HEADS):
        wb_ref[j * tq:(j + 1) * tq, :] = jnp.broadcast_to(w[:, j:j + 1], (tq, LANES))

    n_tiles = (i * tq + tq + tk - 1) // tk
    n_chunks = n_tiles * (tk // LANES)

    hp = IDX_PANEL_HEADS

    def tile(jk, carry):
        koff = pl.multiple_of(jk * tk, tk)
        kt = ik_ref[pl.ds(koff, tk), :]
        acc = jnp.zeros((tq, tk), jnp.float32)
        for pn in range(IDX_HEADS // hp):
            s = lax.dot_general(iqs_ref[pn * hp * tq:(pn + 1) * hp * tq, :], kt, _NT,
                                preferred_element_type=jnp.float32)
            for jj in range(hp):
                j = pn * hp + jj
                wj = wb_ref[j * tq:(j + 1) * tq, :]
                acc = acc + (jnp.maximum(s[jj * tq:(jj + 1) * tq, :], 0.0)
                             * jnp.concatenate([wj] * (tk // LANES), axis=1))
        qpos = i * tq + lax.broadcasted_iota(jnp.int32, (tq, tk), 0)
        kpos = koff + lax.broadcasted_iota(jnp.int32, (tq, tk), 1)
        acc = jnp.where(kpos <= qpos, acc, -jnp.inf)
        bits = pltpu.bitcast(acc, jnp.int32)
        key = bits ^ ((bits >> 31) & 0x7FFFFFFF)
        for part in range(tk // LANES):
            key_ref[jk * (tk // LANES) + part] = key[:, part * LANES:(part + 1) * LANES]
        return carry

    lax.fori_loop(0, n_tiles, tile, 0)

    def search(it, tau):
        cand = tau + jnp.left_shift(jnp.int32(1), 31 - it)

        def count(jk, cnt):
            for part in range(tk // LANES):
                cnt = cnt + jnp.where(key_ref[jk * (tk // LANES) + part] >= cand, 1.0, 0.0)
            return cnt

        cnt = lax.fori_loop(0, n_tiles, count, jnp.zeros((tq, LANES), jnp.float32))
        total = jnp.sum(cnt, axis=-1, keepdims=True)
        return jnp.where(total >= float(k_sel), cand, tau)

    tau = lax.fori_loop(0, 32, search, jnp.full((tq, LANES), -2 ** 31, jnp.int32))

    def emit(c, carry):
        key = key_ref[c]
        keep = (key >= tau) & (key > _KEY_NEG_INF)
        mask_ref[c] = jnp.where(keep, 0.0, NEG).astype(mask_ref.dtype)
        return carry

    lax.fori_loop(0, n_chunks, emit, 0)

    def fill(c, carry):
        mask_ref[c] = jnp.full((tq, LANES), NEG, mask_ref.dtype)
        return carry

    lax.fori_loop(n_chunks, n_chunks_total, fill, 0)


def _dsa_mask(p3, iw3, *, k_sel):
    bsz, seq, _ = p3.shape
    tq = IDX_TQ
    assert seq % IDX_TK == 0
    iq_block = 0
    ik_block = IDX_HEADS
    nck = seq // LANES
    w_scale = (IDX_HEADS ** -0.5) * (IDX_DIM ** -0.5)
    return pl.pallas_call(
        functools.partial(_indexer_body, k_sel=k_sel, w_scale=w_scale),
        grid=(bsz, seq // tq),
        in_specs=[
            pl.BlockSpec((None, tq, IDX_HEADS * IDX_DIM), lambda b, i: (b, i, iq_block)),
            pl.BlockSpec((None, seq, IDX_DIM), lambda b, i: (b, 0, ik_block)),
            pl.BlockSpec((None, tq, LANES), lambda b, i: (b, i, 0)),
        ],
        out_specs=pl.BlockSpec((None, nck, tq, LANES), lambda b, i: (b, 0, i, 0)),
        out_shape=jax.ShapeDtypeStruct((bsz, nck, seq, LANES), jnp.bfloat16),
        scratch_shapes=[
            pltpu.VMEM((IDX_HEADS * tq, IDX_DIM), jnp.bfloat16),
            pltpu.VMEM((IDX_HEADS * tq, LANES), jnp.float32),
            pltpu.VMEM((nck, tq, LANES), jnp.int32),
        ],
        compiler_params=_params("parallel", "arbitrary"),
        name="dsa_indexer",
    )(p3, p3, iw3)


def _attn_body(*refs, kind, n_back, seq):
    t = ATT_TILE
    rows = GROUP * t
    k = 0
    if kind == "swa":
        sink_ref = refs[k]; k += 1
    q_ref, k_ref, v_ref, at_ref = refs[k:k + 4]; k += 4
    if kind == "dsa":
        mask_ref = refs[k]; k += 1
    o_ref = refs[k]; k += 1
    qs_ref, m_ref, acc_ref, s0_ref, s1_ref = refs[k:k + 5]; k += 5
    if kind == "moba":
        kmean_ref = refs[k]

    h = pl.program_id(1)
    i = pl.program_id(2)
    n_delta = at_ref.shape[0]

    for g in range(GROUP):
        qs_ref[g * t:(g + 1) * t, :HEAD_DIM] = q_ref[:, g * HEAD_DIM:(g + 1) * HEAD_DIM]

    if kind == "swa":
        for g in range(GROUP):
            m_ref[g * t:(g + 1) * t, :] = jnp.full((t, LANES), sink_ref[h * GROUP + g] * LOG2E,
                                                   jnp.float32)
        acc_ref[:, :HEAD_DIM] = jnp.zeros((rows, HEAD_DIM), jnp.float32)
        acc_ref[:, HEAD_DIM:] = jnp.ones((rows, LANES), jnp.float32)
    else:
        m_ref[...] = jnp.full_like(m_ref, NEG)
        acc_ref[...] = jnp.zeros_like(acc_ref)

    if kind == "moba":
        n_blk = seq // MOBA_BLOCK

        @pl.when(i == 0)
        def _():
            kf = k_ref[...].astype(jnp.float32).reshape(n_blk, MOBA_BLOCK, HEAD_DIM)
            kmean_ref[...] = jnp.zeros_like(kmean_ref)
            kmean_ref[:n_blk, :] = jnp.sum(kf, axis=1) * (1.0 / MOBA_BLOCK)

        km = kmean_ref[...]
        km_hi = km.astype(jnp.bfloat16)
        km_lo = (km - km_hi.astype(jnp.float32)).astype(jnp.bfloat16)
        q_all = qs_ref[:, :HEAD_DIM]
        gate = (lax.dot_general(q_all, km_hi, _NT, preferred_element_type=jnp.float32)
                + lax.dot_general(q_all, km_lo, _NT, preferred_element_type=jnp.float32))
        lane = lax.broadcasted_iota(jnp.int32, (rows, LANES), 1).astype(jnp.float32)
        i_f = i.astype(jnp.float32)
        gate = jnp.where(lane < i_f, gate, -jnp.inf)
        picked = lane == i_f
        for _ in range(MOBA_TOPK):
            best = jnp.max(gate, axis=-1, keepdims=True)
            first = jnp.min(jnp.where(gate == best, lane, float(LANES)), axis=-1, keepdims=True)
            hit = (lane == first) & (best > -jnp.inf)
            picked = picked | hit
            gate = jnp.where(lane == first, -jnp.inf, gate)
        qs_ref[:, HEAD_DIM:] = jnp.where(picked, 0.0, NEG).astype(jnp.bfloat16)

    def run_tiles(lo, hi, tk, use_at):
        ones = jnp.ones((tk, HEAD_DIM), jnp.bfloat16)

        def logits(j, dst_ref):
            koff = pl.multiple_of(j * tk, tk)
            kt = k_ref[pl.ds(koff, tk), :]
            if kind == "moba":
                lane = lax.broadcasted_iota(jnp.int32, (tk, LANES), 1)
                key = lax.broadcasted_iota(jnp.int32, (tk, LANES), 0)
                blk = j * (tk // MOBA_BLOCK) + key // MOBA_BLOCK
                kt = jnp.concatenate([kt, jnp.where(lane == blk, 1.0, 0.0).astype(jnp.bfloat16)], axis=1)
            dst_ref[:, :tk] = lax.dot_general(qs_ref[...], kt, _NT, preferred_element_type=jnp.float32)

        def softmax_pv(j, src_ref):
            koff = pl.multiple_of(j * tk, tk)
            vt = jnp.concatenate([v_ref[pl.ds(koff, tk), :], ones], axis=1)
            if use_at:
                d = jnp.minimum(i - j, n_delta - 1)
            if kind == "dsa":
                sel = jnp.concatenate([mask_ref[j * (tk // LANES) + part] for part in range(tk // LANES)],
                                      axis=1).astype(jnp.float32)
            for g in range(GROUP):
                r = slice(g * t, (g + 1) * t)
                s = src_ref[r, :tk]
                if use_at:
                    s = s + at_ref[d, r, :]
                if kind == "dsa":
                    s = s + sel
                m_prev = m_ref[r, :]
                m_new = jnp.maximum(m_prev, jnp.max(s, axis=-1, keepdims=True))
                alpha = jnp.exp2(m_prev - m_new)
                p = jnp.exp2(s - jnp.concatenate([m_new] * (tk // LANES), axis=1))
                m_ref[r, :] = m_new
                acc_ref[r, :] = (acc_ref[r, :] * jnp.concatenate([alpha, alpha], axis=1)
                                 + jnp.dot(p.astype(jnp.bfloat16), vt, preferred_element_type=jnp.float32))

        @pl.when(hi > lo)
        def _():
            logits(lo, s0_ref)

        def pair(pi, carry):
            j = lo + 2 * pi
            logits(jnp.minimum(j + 1, hi - 1), s1_ref)
            softmax_pv(j, s0_ref)

            @pl.when(j + 1 < hi)
            def _():
                logits(jnp.minimum(j + 2, hi - 1), s0_ref)
                softmax_pv(j + 1, s1_ref)

            return carry

        lax.fori_loop(0, (hi - lo + 1) // 2, pair, 0)

    if n_back is None:
        n_far = jnp.maximum(i - (n_delta - 2), 0) // 2
        run_tiles(0, n_far, 2 * t, False)
        for g in range(GROUP):
            r = slice(g * t, (g + 1) * t)
            m_ref[r, :] = m_ref[r, :] + at_ref[n_delta - 1, r, :LANES]
        lo = 2 * n_far
    else:
        lo = jnp.maximum(i - n_back, 0)
    run_tiles(lo, i + 1, t, True)

    for g in range(GROUP):
        r = slice(g * t, (g + 1) * t)
        out = acc_ref[r, :HEAD_DIM] / acc_ref[r, HEAD_DIM:]
        o_ref[:, g * HEAD_DIM:(g + 1) * HEAD_DIM] = out.astype(o_ref.dtype)


def _attention(p3, tiles, *, kind, n_heads, n_kv, sinks=None, mask=None):
    bsz, seq, _ = p3.shape
    t = ATT_TILE
    assert seq % t == 0
    rows = GROUP * t
    n_delta = tiles.shape[1]
    n_back = {"swa": n_delta - 1, "dil": n_delta - 1, "dsa": None, "moba": None}[kind]
    qw = GROUP * HEAD_DIM

    in_specs, args = [], []
    if kind == "swa":
        in_specs.append(pl.BlockSpec(memory_space=pltpu.SMEM))
        args.append(sinks.astype(jnp.float32))
    in_specs += [
        pl.BlockSpec((None, t, qw), lambda b, h, i: (b, i, h)),
        pl.BlockSpec((None, seq, HEAD_DIM), lambda b, h, i: (b, 0, n_heads + h)),
        pl.BlockSpec((None, seq, HEAD_DIM), lambda b, h, i: (b, 0, n_heads + n_kv + h)),
        pl.BlockSpec((None, n_delta, rows, t), lambda b, h, i: (h, 0, 0, 0),
                     pipeline_mode=pl.Buffered(1)),
    ]
    args += [p3, p3, p3, tiles]
    if kind == "dsa":
        in_specs.append(pl.BlockSpec((None, seq // LANES, t, LANES), lambda b, h, i: (b, 0, i, 0)))
        args.append(mask)
    q_cols = HEAD_DIM + (LANES if kind == "moba" else 0)
    s_cols = t if n_back is not None else 2 * t
    scratch = [
        pltpu.VMEM((rows, q_cols), jnp.bfloat16),
        pltpu.VMEM((rows, LANES), jnp.float32),
        pltpu.VMEM((rows, HEAD_DIM + LANES), jnp.float32),
        pltpu.VMEM((rows, s_cols), jnp.float32),
        pltpu.VMEM((rows, s_cols), jnp.float32),
    ]
    if kind == "moba":
        assert t == MOBA_BLOCK and seq // MOBA_BLOCK <= LANES
        scratch += [pltpu.VMEM((LANES, HEAD_DIM), jnp.float32)]
    return pl.pallas_call(
        functools.partial(_attn_body, kind=kind, n_back=n_back, seq=seq),
        grid=(bsz, n_kv, seq // t),
        in_specs=in_specs,
        out_specs=pl.BlockSpec((None, t, qw), lambda b, h, i: (b, i, h)),
        out_shape=jax.ShapeDtypeStruct((bsz, seq, n_heads * HEAD_DIM), jnp.bfloat16),
        scratch_shapes=scratch,
        compiler_params=_params("parallel", "parallel", "arbitrary"),
        name="attn_" + kind,
    )(*args)


def _pad_cols(w, mult):
    pad = (-w.shape[-1]) % mult
    if pad:
        w = jnp.pad(w, [(0, 0)] * (w.ndim - 1) + [(0, pad)])
    return w


def kernel(x, rel_bias_table, attn_norm, ffn_norm, final_norm, a_w_in, a_w_out, b_w_in, b_b_in, b_sinks, b_w_out, b_b_out, c_w_in, c_w_out, d_w_in, d_w_out, ffn_w_gate, ffn_w_up, ffn_conv_w, ffn_conv_b, ffn_w_down):
    bsz, seq, d_model = x.shape
    n = bsz * seq
    n_heads = d_model // HEAD_DIM
    n_kv = n_heads // GROUP
    q_w = n_heads * HEAD_DIM
    qkv_w = q_w + 2 * n_kv * HEAD_DIM
    depth = attn_norm.shape[0]
    bf = jnp.bfloat16
    tm = 1024 if n % 1024 == 0 else 512
    tn = 512
    ffn_tf = 256

    causal_t, swa_t, dil_t = _mixer_tiles(rel_bias_table.astype(jnp.float32), n_kv)

    q_fold = jnp.where(jnp.arange(qkv_w) < q_w, (HEAD_DIM ** -0.5) * LOG2E, 1.0).astype(jnp.float32)

    def in_proj(h, w, bias=None, col_scale=None):
        if col_scale is not None:
            w = w * col_scale
            bias = None if bias is None else bias * col_scale
        w = _pad_cols(w.astype(bf), tn)
        if bias is not None:
            bias = _pad_cols(bias, tn)
        return _matmul(h, w, bias, out_dtype=bf, tm=tm, tn=tn)

    x2 = x.reshape(n, d_model)
    for li in range(depth):
        m, j = li % 4, li // 4
        h = _rmsnorm(x2, attn_norm[li], bf)
        if m == 0:
            main_w = qkv_w + IDX_HEADS * IDX_DIM + IDX_DIM
            p = in_proj(h, a_w_in[j][:, :qkv_w], col_scale=q_fold)
            p_idx = in_proj(h, a_w_in[j][:, qkv_w:main_w])
            iw = _matmul(h, _pad_cols(a_w_in[j][:, main_w:].astype(bf), LANES),
                         out_dtype=jnp.float32, tm=tm, tn=LANES)
            p3 = p.reshape(bsz, seq, -1)
            mask = _dsa_mask(p_idx.reshape(bsz, seq, -1), iw.reshape(bsz, seq, LANES),
                             k_sel=min(DSA_TOPK, seq // 4))
            o = _attention(p3, causal_t, kind="dsa", n_heads=n_heads, n_kv=n_kv, mask=mask)
            w_out, b_out = a_w_out[j], None
        elif m == 1:
            p = in_proj(h, b_w_in[j], b_b_in[j], col_scale=q_fold)
            o = _attention(p.reshape(bsz, seq, -1), swa_t, kind="swa", n_heads=n_heads, n_kv=n_kv,
                           sinks=b_sinks[j])
            w_out, b_out = b_w_out[j], b_b_out[j]
        elif m == 2:
            p = in_proj(h, c_w_in[j], col_scale=q_fold)
            o = _attention(p.reshape(bsz, seq, -1), causal_t, kind="moba", n_heads=n_heads, n_kv=n_kv)
            w_out, b_out = c_w_out[j], None
        else:
            p = in_proj(h, d_w_in[j], col_scale=q_fold)
            o = _attention(p.reshape(bsz, seq, -1), dil_t, kind="dil", n_heads=n_heads, n_kv=n_kv)
            w_out, b_out = d_w_out[j], None
        x2 = _matmul(o.reshape(n, q_w), w_out.astype(bf), b_out, x2,
                     out_dtype=jnp.float32, tm=tm, tn=tn)

        h = _rmsnorm(x2, ffn_norm[li], bf)
        wg = _pad_cols(ffn_w_gate[li].astype(bf), ffn_tf)
        wu = _pad_cols(ffn_w_up[li].astype(bf), ffn_tf)
        cw = _pad_cols(ffn_conv_w[li], ffn_tf)
        cb = _pad_cols(ffn_conv_b[li].reshape(1, -1), ffn_tf)
        wd = ffn_w_down[li].astype(bf)
        pad = wg.shape[1] - wd.shape[0]
        if pad:
            wd = jnp.pad(wd, ((0, pad), (0, 0)))
        x2 = _ffn(h, x2, wg, wu, wd, cw, cb, seq=seq, tm=min(512, seq), tf=ffn_tf)

    out = _rmsnorm(x2, final_norm, jnp.float32)
    return out.reshape(bsz, seq, d_model)
```

```python
import functools
import math

import jax
import jax.numpy as jnp
from jax import lax
from jax.experimental import pallas as pl
from jax.experimental.pallas import tpu as pltpu

HEAD_DIM = 128
GROUP = 4
T5_BUCKETS = 32
T5_MAX_DIST = 2048
DSA_TOPK = 256
IDX_HEADS = 32
IDX_DIM = 128
SWA_WINDOW = 128
MOBA_BLOCK = 256
MOBA_TOPK = 3
DILATED_BRANCHES = ((128, 1), (512, 4), (2048, 16))
RMS_EPS = 1e-6

LANES = 128
NEG = -1e30
LOG2E = math.log2(math.e)
ATT_TILE = 256
IDX_TQ = 128
IDX_TK = 256
IDX_PANEL_HEADS = 4
VMEM_LIMIT = 56 * 1024 * 1024

_NT = (((1,), (1,)), ((), ()))


def _params(*sem):
    return pltpu.CompilerParams(dimension_semantics=sem, vmem_limit_bytes=VMEM_LIMIT)


def _rmsnorm_body(x_ref, g_ref, o_ref):
    x = x_ref[...]
    ms = jnp.mean(x * x, axis=-1, keepdims=True)
    y = x * lax.rsqrt(ms + RMS_EPS)
    o_ref[...] = (y * g_ref[...]).astype(o_ref.dtype)


def _rmsnorm(x2d, gain, out_dtype):
    n, d = x2d.shape
    tm = 256
    return pl.pallas_call(
        _rmsnorm_body,
        grid=(n // tm,),
        in_specs=[pl.BlockSpec((tm, d), lambda i: (i, 0)),
                  pl.BlockSpec((1, d), lambda i: (0, 0))],
        out_specs=pl.BlockSpec((tm, d), lambda i: (i, 0)),
        out_shape=jax.ShapeDtypeStruct((n, d), out_dtype),
        compiler_params=_params("parallel"),
        name="rmsnorm",
    )(x2d, gain.reshape(1, d))


NORM_ROWS = 128


def _rms_rows(x_ref, g_ref, dst_ref, dst_row0, n_rows):
    def chunk(c, carry):
        r0 = pl.multiple_of(c * NORM_ROWS, NORM_ROWS)
        x = x_ref[pl.ds(r0, NORM_ROWS), :]
        ms = jnp.mean(x * x, axis=-1, keepdims=True)
        y = x * lax.rsqrt(ms + RMS_EPS)
        dst_ref[pl.ds(dst_row0 + r0, NORM_ROWS), :] = (y * g_ref[...]).astype(dst_ref.dtype)
        return carry

    lax.fori_loop(0, n_rows // NORM_ROWS, chunk, 0)


def _matmul_body(*refs, has_norm, has_bias, has_res, tm):
    k = 0
    if has_norm:
        x_ref, g_ref = refs[0], refs[1]
        hs_ref = refs[-1]
        o_ref = refs[-2]
        k = 2

        @pl.when(pl.program_id(1) == 0)
        def _():
            _rms_rows(x_ref, g_ref, hs_ref, 0, tm)

        a = hs_ref[...]
    else:
        a = refs[0][...]
        o_ref = refs[-1]
        k = 1
    acc = jnp.dot(a, refs[k][...], preferred_element_type=jnp.float32)
    k += 1
    if has_bias:
        acc = acc + refs[k][...]
        k += 1
    if has_res:
        acc = acc + refs[k][...]
    o_ref[...] = acc.astype(o_ref.dtype)


def _matmul(a, w, bias=None, res=None, *, out_dtype, tm, tn, single_buffer_lhs=False, norm_gain=None):
    n, kdim = a.shape
    wn = w.shape[1]
    assert n % tm == 0 and wn % tn == 0 and tm % NORM_ROWS == 0
    has_norm = norm_gain is not None
    lhs_mode = {"pipeline_mode": pl.Buffered(1)} if (single_buffer_lhs or has_norm) else {}
    in_specs = [pl.BlockSpec((tm, kdim), lambda i, j: (i, 0), **lhs_mode)]
    args = [a]
    if has_norm:
        in_specs.append(pl.BlockSpec((1, kdim), lambda i, j: (0, 0)))
        args.append(norm_gain.reshape(1, kdim))
    in_specs.append(pl.BlockSpec((kdim, tn), lambda i, j: (0, j)))
    args.append(w)
    if bias is not None:
        in_specs.append(pl.BlockSpec((1, tn), lambda i, j: (0, j)))
        args.append(bias.reshape(1, wn).astype(jnp.float32))
    if res is not None:
        in_specs.append(pl.BlockSpec((tm, tn), lambda i, j: (i, j)))
        args.append(res)
    return pl.pallas_call(
        functools.partial(_matmul_body, has_norm=has_norm, has_bias=bias is not None,
                          has_res=res is not None, tm=tm),
        grid=(n // tm, wn // tn),
        in_specs=in_specs,
        out_specs=pl.BlockSpec((tm, tn), lambda i, j: (i, j)),
        out_shape=jax.ShapeDtypeStruct((n, wn), out_dtype),
        scratch_shapes=[pltpu.VMEM((tm, kdim), jnp.bfloat16)] if has_norm else [],
        compiler_params=_params("parallel", "arbitrary"),
        name="matmul",
    )(*args)


FFN_HALO = 16


def _ffn_act_body(x_ref, xp_ref, gain_ref, wg_ref, wu_ref, cw_ref, cb_ref, o_ref, hs_ref, *, tm, seq):
    i = pl.program_id(0)
    f = pl.program_id(1)

    @pl.when(f == 0)
    def _():
        _rms_rows(x_ref, gain_ref, hs_ref, FFN_HALO, tm)
        xp = xp_ref[...]
        hp = xp * lax.rsqrt(jnp.mean(xp * xp, axis=-1, keepdims=True) + RMS_EPS) * gain_ref[...]
        first = (i * tm) % seq == 0
        hs_ref[:FFN_HALO, :] = jnp.where(first, jnp.zeros_like(hp), hp).astype(hs_ref.dtype)

    g = jnp.dot(hs_ref[...], wg_ref[...], preferred_element_type=jnp.float32)
    u = jnp.dot(hs_ref[FFN_HALO:, :], wu_ref[...], preferred_element_type=jnp.float32)
    cw = cw_ref[...]
    y = (cb_ref[...]
         + cw[0:1, :] * g[FFN_HALO - 2:FFN_HALO - 2 + tm, :]
         + cw[1:2, :] * g[FFN_HALO - 1:FFN_HALO - 1 + tm, :]
         + cw[2:3, :] * g[FFN_HALO:, :])
    o_ref[...] = ((y * (1.0 / (1.0 + jnp.exp(-y)))) * u).astype(o_ref.dtype)


def _ffn_act(x, gain, wg, wu, cw, cb, *, seq, tm, tf):
    n, d = x.shape
    fdim = wg.shape[1]
    assert n % tm == 0 and fdim % tf == 0 and seq % tm == 0 and tm % FFN_HALO == 0
    hb = tm // FFN_HALO
    return pl.pallas_call(
        functools.partial(_ffn_act_body, tm=tm, seq=seq),
        grid=(n // tm, fdim // tf),
        in_specs=[
            pl.BlockSpec((tm, d), lambda i, f: (i, 0), pipeline_mode=pl.Buffered(1)),
            pl.BlockSpec((FFN_HALO, d), lambda i, f: (jnp.maximum(i * hb - 1, 0), 0)),
            pl.BlockSpec((1, d), lambda i, f: (0, 0)),
            pl.BlockSpec((d, tf), lambda i, f: (0, f)),
            pl.BlockSpec((d, tf), lambda i, f: (0, f)),
            pl.BlockSpec((3, tf), lambda i, f: (0, f)),
            pl.BlockSpec((1, tf), lambda i, f: (0, f)),
        ],
        out_specs=pl.BlockSpec((tm, tf), lambda i, f: (i, f)),
        out_shape=jax.ShapeDtypeStruct((n, fdim), jnp.bfloat16),
        scratch_shapes=[pltpu.VMEM((FFN_HALO + tm, d), jnp.bfloat16)],
        compiler_params=_params("parallel", "arbitrary"),
        name="ffn_act",
    )(x, x, gain.reshape(1, d), wg, wu, cw, cb)


def _t5_bucket(dist):
    n = jnp.maximum(dist, 0)
    max_exact = T5_BUCKETS // 2
    nf = jnp.maximum(n, 1).astype(jnp.float32)
    large = max_exact + (jnp.log(nf / max_exact) / math.log(T5_MAX_DIST / max_exact)
                         * (T5_BUCKETS - max_exact)).astype(jnp.int32)
    large = jnp.minimum(large, T5_BUCKETS - 1)
    return jnp.where(n < max_exact, n, large)


def _tile_dist(n_delta):
    t = ATT_TILE
    d = jnp.arange(n_delta, dtype=jnp.int32)[:, None, None] * t
    qi = jnp.arange(t, dtype=jnp.int32)[None, :, None]
    kj = jnp.arange(t, dtype=jnp.int32)[None, None, :]
    return d + qi - kj


def _at_body(tab_ref, bkt_ref, add_ref, o_ref, *, n_heads):
    head = pl.program_id(0) * GROUP + pl.program_id(2)
    bkt = bkt_ref[...]
    val = jnp.zeros(bkt.shape, jnp.float32)
    for b in range(T5_BUCKETS):
        val = jnp.where(bkt == b, tab_ref[b * n_heads + head], val)
    o_ref[...] = (val + add_ref[...]) * LOG2E


def _additive_tiles(table, bucket, add, n_kv):
    n_delta, t, _ = bucket.shape
    n_heads = n_kv * GROUP
    out = pl.pallas_call(
        functools.partial(_at_body, n_heads=n_heads),
        grid=(n_kv, n_delta, GROUP),
        in_specs=[pl.BlockSpec(memory_space=pltpu.SMEM),
                  pl.BlockSpec((None, t, t), lambda h, d, g: (d, 0, 0)),
                  pl.BlockSpec((None, t, t), lambda h, d, g: (d, 0, 0))],
        out_specs=pl.BlockSpec((None, None, None, t, t), lambda h, d, g: (h, d, g, 0, 0)),
        out_shape=jax.ShapeDtypeStruct((n_kv, n_delta, GROUP, t, t), jnp.float32),
        compiler_params=_params("parallel", "parallel", "parallel"),
        name="bias_tiles",
    )(table.reshape(-1), bucket, add)
    return out.reshape(n_kv, n_delta, GROUP * t, t)


def _mixer_tiles(table, n_kv):
    max_exact = T5_BUCKETS // 2
    far = math.ceil(max_exact * (T5_MAX_DIST / max_exact)
                    ** ((T5_BUCKETS - 1 - max_exact) / (T5_BUCKETS - max_exact))) + 2
    nd_causal = -(-(far + ATT_TILE - 1) // ATT_TILE) + 1
    dist = _tile_dist(nd_causal)
    causal = _additive_tiles(table, _t5_bucket(dist),
                             jnp.where(dist >= 0, 0.0, NEG).astype(jnp.float32), n_kv)

    nd_swa = (SWA_WINDOW - 1 + ATT_TILE - 1) // ATT_TILE + 1
    dist = _tile_dist(nd_swa)
    swa = _additive_tiles(table, _t5_bucket(dist),
                          jnp.where((dist >= 0) & (dist < SWA_WINDOW), 0.0, NEG).astype(jnp.float32),
                          n_kv)

    max_w = max(w for w, _ in DILATED_BRANCHES)
    nd_dil = (max_w + ATT_TILE - 1) // ATT_TILE + 1
    dist = _tile_dist(nd_dil)
    count = jnp.zeros(dist.shape, jnp.int32)
    for w, r in DILATED_BRANCHES:
        count = count + ((dist >= 0) & (dist <= w) & (dist % r == 0)).astype(jnp.int32)
    add = jnp.where(count > 0, jnp.log(jnp.maximum(count, 1).astype(jnp.float32)), NEG)
    dil = _additive_tiles(table, _t5_bucket(dist), add.astype(jnp.float32), n_kv)
    return causal, swa, dil


_KEY_NEG_INF = -2139095041


def _indexer_body(iq_ref, ik_ref, iw_ref, mask_ref, iqs_ref, wb_ref, key_ref, *, k_sel, w_scale):
    tq, tk = IDX_TQ, IDX_TK
    i = pl.program_id(1)
    n_chunks_total = key_ref.shape[0]

    for j in range(IDX_HEADS):
        iqs_ref[j * tq:(j + 1) * tq, :] = iq_ref[:, j * IDX_DIM:(j + 1) * IDX_DIM]
    w = iw_ref[...] * w_scale
    for j in range(IDX_HEADS):
        wb_ref[j * tq:(j + 1) * tq, :] = jnp.broadcast_to(w[:, j:j + 1], (tq, LANES))

    n_tiles = (i * tq + tq + tk - 1) // tk
    n_chunks = n_tiles * (tk // LANES)

    hp = IDX_PANEL_HEADS

    def tile(jk, carry):
        koff = pl.multiple_of(jk * tk, tk)
        kt = ik_ref[pl.ds(koff, tk), :]
        acc = jnp.zeros((tq, tk), jnp.float32)
        for pn in range(IDX_HEADS // hp):
            s = lax.dot_general(iqs_ref[pn * hp * tq:(pn + 1) * hp * tq, :], kt, _NT,
                                preferred_element_type=jnp.float32)
            for jj in range(hp):
                j = pn * hp + jj
                wj = wb_ref[j * tq:(j + 1) * tq, :]
                acc = acc + (jnp.maximum(s[jj * tq:(jj + 1) * tq, :], 0.0)
                             * jnp.concatenate([wj] * (tk // LANES), axis=1))
        qpos = i * tq + lax.broadcasted_iota(jnp.int32, (tq, tk), 0)
        kpos = koff + lax.broadcasted_iota(jnp.int32, (tq, tk), 1)
        acc = jnp.where(kpos <= qpos, acc, -jnp.inf)
        bits = pltpu.bitcast(acc, jnp.int32)
        key = bits ^ ((bits >> 31) & 0x7FFFFFFF)
        for part in range(tk // LANES):
            key_ref[jk * (tk // LANES) + part] = key[:, part * LANES:(part + 1) * LANES]
        return carry

    lax.fori_loop(0, n_tiles, tile, 0)

    def search(it, tau):
        cand = tau + jnp.left_shift(jnp.int32(1), 31 - it)

        def count(jk, cnt):
            for part in range(tk // LANES):
                cnt = cnt + jnp.where(key_ref[jk * (tk // LANES) + part] >= cand, 1.0, 0.0)
            return cnt

        cnt = lax.fori_loop(0, n_tiles, count, jnp.zeros((tq, LANES), jnp.float32))
        total = jnp.sum(cnt, axis=-1, keepdims=True)
        return jnp.where(total >= float(k_sel), cand, tau)

    tau = lax.fori_loop(0, 32, search, jnp.full((tq, LANES), -2 ** 31, jnp.int32))

    def emit(c, carry):
        key = key_ref[c]
        keep = (key >= tau) & (key > _KEY_NEG_INF)
        mask_ref[c] = jnp.where(keep, 0.0, NEG).astype(mask_ref.dtype)
        return carry

    lax.fori_loop(0, n_chunks, emit, 0)

    def fill(c, carry):
        mask_ref[c] = jnp.full((tq, LANES), NEG, mask_ref.dtype)
        return carry

    lax.fori_loop(n_chunks, n_chunks_total, fill, 0)


def _dsa_mask(p3, iw3, *, k_sel):
    bsz, seq, _ = p3.shape
    tq = IDX_TQ
    assert seq % IDX_TK == 0
    iq_block = 0
    ik_block = IDX_HEADS
    nck = seq // LANES
    w_scale = (IDX_HEADS ** -0.5) * (IDX_DIM ** -0.5)
    return pl.pallas_call(
        functools.partial(_indexer_body, k_sel=k_sel, w_scale=w_scale),
        grid=(bsz, seq // tq),
        in_specs=[
            pl.BlockSpec((None, tq, IDX_HEADS * IDX_DIM), lambda b, i: (b, i, iq_block)),
            pl.BlockSpec((None, seq, IDX_DIM), lambda b, i: (b, 0, ik_block)),
            pl.BlockSpec((None, tq, LANES), lambda b, i: (b, i, 0)),
        ],
        out_specs=pl.BlockSpec((None, nck, tq, LANES), lambda b, i: (b, 0, i, 0)),
        out_shape=jax.ShapeDtypeStruct((bsz, nck, seq, LANES), jnp.bfloat16),
        scratch_shapes=[
            pltpu.VMEM((IDX_HEADS * tq, IDX_DIM), jnp.bfloat16),
            pltpu.VMEM((IDX_HEADS * tq, LANES), jnp.float32),
            pltpu.VMEM((nck, tq, LANES), jnp.int32),
        ],
        compiler_params=_params("parallel", "arbitrary"),
        name="dsa_indexer",
    )(p3, p3, iw3)


def _attn_body(*refs, kind, n_back, seq):
    t = ATT_TILE
    rows = GROUP * t
    k = 0
    if kind == "swa":
        sink_ref = refs[k]; k += 1
    q_ref, k_ref, v_ref, at_ref = refs[k:k + 4]; k += 4
    if kind == "dsa":
        mask_ref = refs[k]; k += 1
    o_ref = refs[k]; k += 1
    qs_ref, m_ref, acc_ref, s0_ref, s1_ref = refs[k:k + 5]; k += 5
    if kind == "moba":
        kmean_ref = refs[k]

    h = pl.program_id(1)
    i = pl.program_id(2)
    n_delta = at_ref.shape[0]

    for g in range(GROUP):
        qs_ref[g * t:(g + 1) * t, :HEAD_DIM] = q_ref[:, g * HEAD_DIM:(g + 1) * HEAD_DIM]

    if kind == "swa":
        for g in range(GROUP):
            m_ref[g * t:(g + 1) * t, :] = jnp.full((t, LANES), sink_ref[h * GROUP + g] * LOG2E,
                                                   jnp.float32)
        acc_ref[:, :HEAD_DIM] = jnp.zeros((rows, HEAD_DIM), jnp.float32)
        acc_ref[:, HEAD_DIM:] = jnp.ones((rows, LANES), jnp.float32)
    else:
        m_ref[...] = jnp.full_like(m_ref, NEG)
        acc_ref[...] = jnp.zeros_like(acc_ref)

    if kind == "moba":
        n_blk = seq // MOBA_BLOCK

        @pl.when(i == 0)
        def _():
            kf = k_ref[...].astype(jnp.float32).reshape(n_blk, MOBA_BLOCK, HEAD_DIM)
            kmean_ref[...] = jnp.zeros_like(kmean_ref)
            kmean_ref[:n_blk, :] = jnp.sum(kf, axis=1) * (1.0 / MOBA_BLOCK)

        nb = -(-n_blk // 8) * 8
        km = kmean_ref[:nb, :]
        km_hi = km.astype(jnp.bfloat16)
        km_lo = (km - km_hi.astype(jnp.float32)).astype(jnp.bfloat16)
        q_all = qs_ref[:, :HEAD_DIM]
        gate = (lax.dot_general(km_hi, q_all, _NT, preferred_element_type=jnp.float32)
                + lax.dot_general(km_lo, q_all, _NT, preferred_element_type=jnp.float32))
        blk = lax.broadcasted_iota(jnp.int32, (nb, rows), 0).astype(jnp.float32)
        i_f = i.astype(jnp.float32)
        gate = jnp.where(blk < i_f, gate, -jnp.inf)
        picked = blk == i_f
        for _ in range(MOBA_TOPK):
            best = jnp.max(gate, axis=0, keepdims=True)
            first = jnp.min(jnp.where(gate == best, blk, float(LANES)), axis=0, keepdims=True)
            hit = (blk == first) & (best > -jnp.inf)
            picked = picked | hit
            gate = jnp.where(blk == first, -jnp.inf, gate)
        sel = jnp.concatenate([jnp.where(picked, 0.0, NEG),
                               jnp.full((LANES - nb, rows), NEG, jnp.float32)], axis=0)
        qs_ref[:, HEAD_DIM:] = sel.T.astype(jnp.bfloat16)

    def run_tiles(lo, hi, tk, use_at):
        ones = jnp.ones((tk, HEAD_DIM), jnp.bfloat16)

        def logits(j, dst_ref):
            koff = pl.multiple_of(j * tk, tk)
            kt = k_ref[pl.ds(koff, tk), :]
            if kind == "moba":
                lane = lax.broadcasted_iota(jnp.int32, (tk, LANES), 1)
                key = lax.broadcasted_iota(jnp.int32, (tk, LANES), 0)
                blk = j * (tk // MOBA_BLOCK) + key // MOBA_BLOCK
                kt = jnp.concatenate([kt, jnp.where(lane == blk, 1.0, 0.0).astype(jnp.bfloat16)], axis=1)
            dst_ref[:, :tk] = lax.dot_general(qs_ref[...], kt, _NT, preferred_element_type=jnp.float32)

        def softmax_pv(j, src_ref):
            koff = pl.multiple_of(j * tk, tk)
            vt = jnp.concatenate([v_ref[pl.ds(koff, tk), :], ones], axis=1)
            if use_at:
                d = jnp.minimum(i - j, n_delta - 1)
            if kind == "dsa":
                sel = jnp.concatenate([mask_ref[j * (tk // LANES) + part] for part in range(tk // LANES)],
                                      axis=1).astype(jnp.float32)
            for g in range(GROUP):
                r = slice(g * t, (g + 1) * t)
                s = src_ref[r, :tk]
                if use_at:
                    s = s + at_ref[d, r, :]
                if kind == "dsa":
                    s = s + sel
                m_prev = m_ref[r, :]
                m_new = jnp.maximum(m_prev, jnp.max(s, axis=-1, keepdims=True))
                alpha = jnp.exp2(m_prev - m_new)
                p = jnp.exp2(s - jnp.concatenate([m_new] * (tk // LANES), axis=1))
                m_ref[r, :] = m_new
                acc_ref[r, :] = (acc_ref[r, :] * jnp.concatenate([alpha, alpha], axis=1)
                                 + jnp.dot(p.astype(jnp.bfloat16), vt, preferred_element_type=jnp.float32))

        @pl.when(hi > lo)
        def _():
            logits(lo, s0_ref)

        def pair(pi, carry):
            j = lo + 2 * pi
            logits(jnp.minimum(j + 1, hi - 1), s1_ref)
            softmax_pv(j, s0_ref)

            @pl.when(j + 1 < hi)
            def _():
                logits(jnp.minimum(j + 2, hi - 1), s0_ref)
                softmax_pv(j + 1, s1_ref)

            return carry

        lax.fori_loop(0, (hi - lo + 1) // 2, pair, 0)

    if n_back is None:
        n_far = jnp.maximum(i - (n_delta - 2), 0) // 2
        run_tiles(0, n_far, 2 * t, False)
        for g in range(GROUP):
            r = slice(g * t, (g + 1) * t)
            m_ref[r, :] = m_ref[r, :] + at_ref[n_delta - 1, r, :LANES]
        lo = 2 * n_far
    else:
        lo = jnp.maximum(i - n_back, 0)
    run_tiles(lo, i + 1, t, True)

    for g in range(GROUP):
        r = slice(g * t, (g + 1) * t)
        out = acc_ref[r, :HEAD_DIM] / acc_ref[r, HEAD_DIM:]
        o_ref[:, g * HEAD_DIM:(g + 1) * HEAD_DIM] = out.astype(o_ref.dtype)


def _attention(p3, tiles, *, kind, n_heads, n_kv, sinks=None, mask=None):
    bsz, seq, _ = p3.shape
    t = ATT_TILE
    assert seq % t == 0
    rows = GROUP * t
    n_delta = tiles.shape[1]
    n_back = {"swa": n_delta - 1, "dil": n_delta - 1, "dsa": None, "moba": None}[kind]
    qw = GROUP * HEAD_DIM

    in_specs, args = [], []
    if kind == "swa":
        in_specs.append(pl.BlockSpec(memory_space=pltpu.SMEM))
        args.append(sinks.astype(jnp.float32))
    in_specs += [
        pl.BlockSpec((None, t, qw), lambda b, h, i: (b, i, h)),
        pl.BlockSpec((None, seq, HEAD_DIM), lambda b, h, i: (b, 0, n_heads + h)),
        pl.BlockSpec((None, seq, HEAD_DIM), lambda b, h, i: (b, 0, n_heads + n_kv + h)),
        pl.BlockSpec((None, n_delta, rows, t), lambda b, h, i: (h, 0, 0, 0),
                     pipeline_mode=pl.Buffered(1)),
    ]
    args += [p3, p3, p3, tiles]
    if kind == "dsa":
        in_specs.append(pl.BlockSpec((None, seq // LANES, t, LANES), lambda b, h, i: (b, 0, i, 0)))
        args.append(mask)
    q_cols = HEAD_DIM + (LANES if kind == "moba" else 0)
    s_cols = t if n_back is not None else 2 * t
    scratch = [
        pltpu.VMEM((rows, q_cols), jnp.bfloat16),
        pltpu.VMEM((rows, LANES), jnp.float32),
        pltpu.VMEM((rows, HEAD_DIM + LANES), jnp.float32),
        pltpu.VMEM((rows, s_cols), jnp.float32),
        pltpu.VMEM((rows, s_cols), jnp.float32),
    ]
    if kind == "moba":
        assert t == MOBA_BLOCK and seq // MOBA_BLOCK <= LANES
        scratch += [pltpu.VMEM((LANES, HEAD_DIM), jnp.float32)]
    return pl.pallas_call(
        functools.partial(_attn_body, kind=kind, n_back=n_back, seq=seq),
        grid=(bsz, n_kv, seq // t),
        in_specs=in_specs,
        out_specs=pl.BlockSpec((None, t, qw), lambda b, h, i: (b, i, h)),
        out_shape=jax.ShapeDtypeStruct((bsz, seq, n_heads * HEAD_DIM), jnp.bfloat16),
        scratch_shapes=scratch,
        compiler_params=_params("parallel", "parallel", "arbitrary"),
        name="attn_" + kind,
    )(*args)


def _pad_cols(w, mult):
    pad = (-w.shape[-1]) % mult
    if pad:
        w = jnp.pad(w, [(0, 0)] * (w.ndim - 1) + [(0, pad)])
    return w


def kernel(x, rel_bias_table, attn_norm, ffn_norm, final_norm, a_w_in, a_w_out, b_w_in, b_b_in, b_sinks, b_w_out, b_b_out, c_w_in, c_w_out, d_w_in, d_w_out, ffn_w_gate, ffn_w_up, ffn_conv_w, ffn_conv_b, ffn_w_down):
    bsz, seq, d_model = x.shape
    n = bsz * seq
    n_heads = d_model // HEAD_DIM
    n_kv = n_heads // GROUP
    q_w = n_heads * HEAD_DIM
    qkv_w = q_w + 2 * n_kv * HEAD_DIM
    depth = attn_norm.shape[0]
    bf = jnp.bfloat16
    tm = 1024 if n % 1024 == 0 else 512
    tn = 512
    ffn_tf = 512

    causal_t, swa_t, dil_t = _mixer_tiles(rel_bias_table.astype(jnp.float32), n_kv)

    q_fold = jnp.where(jnp.arange(qkv_w) < q_w, (HEAD_DIM ** -0.5) * LOG2E, 1.0).astype(jnp.float32)

    def in_proj(x2, gain, w, bias=None, col_scale=None):
        if col_scale is not None:
            w = w * col_scale
            bias = None if bias is None else bias * col_scale
        w = _pad_cols(w.astype(bf), tn)
        if bias is not None:
            bias = _pad_cols(bias, tn)
        return _matmul(x2, w, bias, out_dtype=bf, tm=tm, tn=tn, norm_gain=gain)

    x2 = x.reshape(n, d_model)
    for li in range(depth):
        m, j = li % 4, li // 4
        gain = attn_norm[li]
        if m == 0:
            main_w = qkv_w + IDX_HEADS * IDX_DIM + IDX_DIM
            p = in_proj(x2, gain, a_w_in[j][:, :qkv_w], col_scale=q_fold)
            p_idx = in_proj(x2, gain, a_w_in[j][:, qkv_w:main_w])
            iw = _matmul(x2, _pad_cols(a_w_in[j][:, main_w:].astype(bf), LANES),
                         out_dtype=jnp.float32, tm=tm, tn=LANES, norm_gain=gain)
            p3 = p.reshape(bsz, seq, -1)
            mask = _dsa_mask(p_idx.reshape(bsz, seq, -1), iw.reshape(bsz, seq, LANES),
                             k_sel=min(DSA_TOPK, seq // 4))
            o = _attention(p3, causal_t, kind="dsa", n_heads=n_heads, n_kv=n_kv, mask=mask)
            w_out, b_out = a_w_out[j], None
        elif m == 1:
            p = in_proj(x2, gain, b_w_in[j], b_b_in[j], col_scale=q_fold)
            o = _attention(p.reshape(bsz, seq, -1), swa_t, kind="swa", n_heads=n_heads, n_kv=n_kv,
                           sinks=b_sinks[j])
            w_out, b_out = b_w_out[j], b_b_out[j]
        elif m == 2:
            p = in_proj(x2, gain, c_w_in[j], col_scale=q_fold)
            o = _attention(p.reshape(bsz, seq, -1), causal_t, kind="moba", n_heads=n_heads, n_kv=n_kv)
            w_out, b_out = c_w_out[j], None
        else:
            p = in_proj(x2, gain, d_w_in[j], col_scale=q_fold)
            o = _attention(p.reshape(bsz, seq, -1), dil_t, kind="dil", n_heads=n_heads, n_kv=n_kv)
            w_out, b_out = d_w_out[j], None
        x2 = _matmul(o.reshape(n, q_w), w_out.astype(bf), b_out, x2,
                     out_dtype=jnp.float32, tm=tm, tn=tn)

        wg = _pad_cols(ffn_w_gate[li].astype(bf), ffn_tf)
        wu = _pad_cols(ffn_w_up[li].astype(bf), ffn_tf)
        cw = _pad_cols(ffn_conv_w[li], ffn_tf)
        cb = _pad_cols(ffn_conv_b[li].reshape(1, -1), ffn_tf)
        wd = ffn_w_down[li].astype(bf)
        pad = wg.shape[1] - wd.shape[0]
        if pad:
            wd = jnp.pad(wd, ((0, pad), (0, 0)))
        act = _ffn_act(x2, ffn_norm[li], wg, wu, cw, cb, seq=seq, tm=min(tm, seq), tf=ffn_tf)
        x2 = _matmul(act, wd, None, x2, out_dtype=jnp.float32, tm=tm, tn=256, single_buffer_lhs=True)

    out = _rmsnorm(x2, final_norm, jnp.float32)
    return out.reshape(bsz, seq, d_model)
```

```python
import functools
import math

import jax
import jax.numpy as jnp
from jax import lax
from jax.experimental import pallas as pl
from jax.experimental.pallas import tpu as pltpu

HEAD_DIM = 128
GROUP = 4
T5_BUCKETS = 32
T5_MAX_DIST = 2048
DSA_TOPK = 256
IDX_HEADS = 32
IDX_DIM = 128
SWA_WINDOW = 128
MOBA_BLOCK = 256
MOBA_TOPK = 3
DILATED_BRANCHES = ((128, 1), (512, 4), (2048, 16))
RMS_EPS = 1e-6

LANES = 128
NEG = -1e30
LOG2E = math.log2(math.e)
ATT_TILE = 256
IDX_TQ = 128
IDX_TK = 256
IDX_PANEL_HEADS = 4
VMEM_LIMIT = 56 * 1024 * 1024

_NT = (((1,), (1,)), ((), ()))


def _params(*sem):
    return pltpu.CompilerParams(dimension_semantics=sem, vmem_limit_bytes=VMEM_LIMIT)


def _rmsnorm_body(x_ref, g_ref, o_ref):
    x = x_ref[...]
    ms = jnp.mean(x * x, axis=-1, keepdims=True)
    y = x * lax.rsqrt(ms + RMS_EPS)
    o_ref[...] = (y * g_ref[...]).astype(o_ref.dtype)


def _rmsnorm(x2d, gain, out_dtype):
    n, d = x2d.shape
    tm = 256
    return pl.pallas_call(
        _rmsnorm_body,
        grid=(n // tm,),
        in_specs=[pl.BlockSpec((tm, d), lambda i: (i, 0)),
                  pl.BlockSpec((1, d), lambda i: (0, 0))],
        out_specs=pl.BlockSpec((tm, d), lambda i: (i, 0)),
        out_shape=jax.ShapeDtypeStruct((n, d), out_dtype),
        compiler_params=_params("parallel"),
        name="rmsnorm",
    )(x2d, gain.reshape(1, d))


NORM_ROWS = 16


def _rms_rows(x_ref, g_ref, dst_ref, dst_row0, n_rows):
    def chunk(c, carry):
        r0 = pl.multiple_of(c * NORM_ROWS, NORM_ROWS)
        x = x_ref[pl.ds(r0, NORM_ROWS), :]
        ms = jnp.mean(x * x, axis=-1, keepdims=True)
        y = x * lax.rsqrt(ms + RMS_EPS)
        dst_ref[pl.ds(dst_row0 + r0, NORM_ROWS), :] = (y * g_ref[...]).astype(dst_ref.dtype)
        return carry

    lax.fori_loop(0, n_rows // NORM_ROWS, chunk, 0, unroll=4)


def _matmul_body(*refs, has_norm, has_bias, has_res, tm):
    k = 0
    if has_norm:
        x_ref, g_ref = refs[0], refs[1]
        hs_ref = refs[-1]
        o_ref = refs[-2]
        k = 2

        @pl.when(pl.program_id(1) == 0)
        def _():
            _rms_rows(x_ref, g_ref, hs_ref, 0, tm)

        a = hs_ref[...]
    else:
        a = refs[0][...]
        o_ref = refs[-1]
        k = 1
    acc = jnp.dot(a, refs[k][...], preferred_element_type=jnp.float32)
    k += 1
    if has_bias:
        acc = acc + refs[k][...]
        k += 1
    if has_res:
        acc = acc + refs[k][...]
    o_ref[...] = acc.astype(o_ref.dtype)


def _matmul(a, w, bias=None, res=None, *, out_dtype, tm, tn, single_buffer_lhs=False, norm_gain=None):
    n, kdim = a.shape
    wn = w.shape[1]
    assert n % tm == 0 and wn % tn == 0 and tm % NORM_ROWS == 0
    has_norm = norm_gain is not None
    lhs_mode = {"pipeline_mode": pl.Buffered(1)} if single_buffer_lhs else {}
    in_specs = [pl.BlockSpec((tm, kdim), lambda i, j: (i, 0), **lhs_mode)]
    args = [a]
    if has_norm:
        in_specs.append(pl.BlockSpec((1, kdim), lambda i, j: (0, 0)))
        args.append(norm_gain.reshape(1, kdim))
    in_specs.append(pl.BlockSpec((kdim, tn), lambda i, j: (0, j)))
    args.append(w)
    if bias is not None:
        in_specs.append(pl.BlockSpec((1, tn), lambda i, j: (0, j)))
        args.append(bias.reshape(1, wn).astype(jnp.float32))
    if res is not None:
        in_specs.append(pl.BlockSpec((tm, tn), lambda i, j: (i, j)))
        args.append(res)
    return pl.pallas_call(
        functools.partial(_matmul_body, has_norm=has_norm, has_bias=bias is not None,
                          has_res=res is not None, tm=tm),
        grid=(n // tm, wn // tn),
        in_specs=in_specs,
        out_specs=pl.BlockSpec((tm, tn), lambda i, j: (i, j)),
        out_shape=jax.ShapeDtypeStruct((n, wn), out_dtype),
        scratch_shapes=[pltpu.VMEM((tm, kdim), jnp.bfloat16)] if has_norm else [],
        compiler_params=_params("parallel", "arbitrary"),
        name="matmul",
    )(*args)


FFN_HALO = 16


def _ffn_act_body(x_ref, xp_ref, gain_ref, wg_ref, wu_ref, cw_ref, cb_ref, o_ref, hs_ref, *, tm, seq):
    i = pl.program_id(0)
    f = pl.program_id(1)

    @pl.when(f == 0)
    def _():
        _rms_rows(x_ref, gain_ref, hs_ref, FFN_HALO, tm)
        xp = xp_ref[...]
        hp = xp * lax.rsqrt(jnp.mean(xp * xp, axis=-1, keepdims=True) + RMS_EPS) * gain_ref[...]
        first = (i * tm) % seq == 0
        hs_ref[:FFN_HALO, :] = jnp.where(first, jnp.zeros_like(hp), hp).astype(hs_ref.dtype)

    g = jnp.dot(hs_ref[...], wg_ref[...], preferred_element_type=jnp.float32)
    u = jnp.dot(hs_ref[FFN_HALO:, :], wu_ref[...], preferred_element_type=jnp.float32)
    cw = cw_ref[...]
    y = (cb_ref[...]
         + cw[0:1, :] * g[FFN_HALO - 2:FFN_HALO - 2 + tm, :]
         + cw[1:2, :] * g[FFN_HALO - 1:FFN_HALO - 1 + tm, :]
         + cw[2:3, :] * g[FFN_HALO:, :])
    o_ref[...] = ((y * (1.0 / (1.0 + jnp.exp(-y)))) * u).astype(o_ref.dtype)


def _ffn_act(x, gain, wg, wu, cw, cb, *, seq, tm, tf):
    n, d = x.shape
    fdim = wg.shape[1]
    assert n % tm == 0 and fdim % tf == 0 and seq % tm == 0 and tm % FFN_HALO == 0
    hb = tm // FFN_HALO
    return pl.pallas_call(
        functools.partial(_ffn_act_body, tm=tm, seq=seq),
        grid=(n // tm, fdim // tf),
        in_specs=[
            pl.BlockSpec((tm, d), lambda i, f: (i, 0), pipeline_mode=pl.Buffered(1)),
            pl.BlockSpec((FFN_HALO, d), lambda i, f: (jnp.maximum(i * hb - 1, 0), 0)),
            pl.BlockSpec((1, d), lambda i, f: (0, 0)),
            pl.BlockSpec((d, tf), lambda i, f: (0, f)),
            pl.BlockSpec((d, tf), lambda i, f: (0, f)),
            pl.BlockSpec((3, tf), lambda i, f: (0, f)),
            pl.BlockSpec((1, tf), lambda i, f: (0, f)),
        ],
        out_specs=pl.BlockSpec((tm, tf), lambda i, f: (i, f)),
        out_shape=jax.ShapeDtypeStruct((n, fdim), jnp.bfloat16),
        scratch_shapes=[pltpu.VMEM((FFN_HALO + tm, d), jnp.bfloat16)],
        compiler_params=_params("parallel", "arbitrary"),
        name="ffn_act",
    )(x, x, gain.reshape(1, d), wg, wu, cw, cb)


def _t5_bucket(dist):
    n = jnp.maximum(dist, 0)
    max_exact = T5_BUCKETS // 2
    nf = jnp.maximum(n, 1).astype(jnp.float32)
    large = max_exact + (jnp.log(nf / max_exact) / math.log(T5_MAX_DIST / max_exact)
                         * (T5_BUCKETS - max_exact)).astype(jnp.int32)
    large = jnp.minimum(large, T5_BUCKETS - 1)
    return jnp.where(n < max_exact, n, large)


def _tile_dist(n_delta):
    t = ATT_TILE
    d = jnp.arange(n_delta, dtype=jnp.int32)[:, None, None] * t
    qi = jnp.arange(t, dtype=jnp.int32)[None, :, None]
    kj = jnp.arange(t, dtype=jnp.int32)[None, None, :]
    return d + qi - kj


def _at_body(tab_ref, bkt_ref, add_ref, o_ref, *, n_heads):
    head = pl.program_id(0) * GROUP + pl.program_id(2)
    bkt = bkt_ref[...]
    val = jnp.zeros(bkt.shape, jnp.float32)
    for b in range(T5_BUCKETS):
        val = jnp.where(bkt == b, tab_ref[b * n_heads + head], val)
    o_ref[...] = (val + add_ref[...]) * LOG2E


def _additive_tiles(table, bucket, add, n_kv):
    n_delta, t, _ = bucket.shape
    n_heads = n_kv * GROUP
    out = pl.pallas_call(
        functools.partial(_at_body, n_heads=n_heads),
        grid=(n_kv, n_delta, GROUP),
        in_specs=[pl.BlockSpec(memory_space=pltpu.SMEM),
                  pl.BlockSpec((None, t, t), lambda h, d, g: (d, 0, 0)),
                  pl.BlockSpec((None, t, t), lambda h, d, g: (d, 0, 0))],
        out_specs=pl.BlockSpec((None, None, None, t, t), lambda h, d, g: (h, d, g, 0, 0)),
        out_shape=jax.ShapeDtypeStruct((n_kv, n_delta, GROUP, t, t), jnp.float32),
        compiler_params=_params("parallel", "parallel", "parallel"),
        name="bias_tiles",
    )(table.reshape(-1), bucket, add)
    return out.reshape(n_kv, n_delta, GROUP * t, t)


def _mixer_tiles(table, n_kv):
    max_exact = T5_BUCKETS // 2
    far = math.ceil(max_exact * (T5_MAX_DIST / max_exact)
                    ** ((T5_BUCKETS - 1 - max_exact) / (T5_BUCKETS - max_exact))) + 2
    nd_causal = -(-(far + ATT_TILE - 1) // ATT_TILE) + 1
    dist = _tile_dist(nd_causal)
    causal = _additive_tiles(table, _t5_bucket(dist),
                             jnp.where(dist >= 0, 0.0, NEG).astype(jnp.float32), n_kv)

    nd_swa = (SWA_WINDOW - 1 + ATT_TILE - 1) // ATT_TILE + 1
    dist = _tile_dist(nd_swa)
    swa = _additive_tiles(table, _t5_bucket(dist),
                          jnp.where((dist >= 0) & (dist < SWA_WINDOW), 0.0, NEG).astype(jnp.float32),
                          n_kv)

    max_w = max(w for w, _ in DILATED_BRANCHES)
    nd_dil = (max_w + ATT_TILE - 1) // ATT_TILE + 1
    dist = _tile_dist(nd_dil)
    count = jnp.zeros(dist.shape, jnp.int32)
    for w, r in DILATED_BRANCHES:
        count = count + ((dist >= 0) & (dist <= w) & (dist % r == 0)).astype(jnp.int32)
    add = jnp.where(count > 0, jnp.log(jnp.maximum(count, 1).astype(jnp.float32)), NEG)
    dil = _additive_tiles(table, _t5_bucket(dist), add.astype(jnp.float32), n_kv)
    return causal, swa, dil


_KEY_NEG_INF = -2139095041


def _indexer_body(iq_ref, ik_ref, iw_ref, mask_ref, iqs_ref, wb_ref, key_ref, *, k_sel, w_scale):
    tq, tk = IDX_TQ, IDX_TK
    i = pl.program_id(1)
    n_chunks_total = key_ref.shape[0]

    for j in range(IDX_HEADS):
        iqs_ref[j * tq:(j + 1) * tq, :] = iq_ref[:, j * IDX_DIM:(j + 1) * IDX_DIM]
    w = iw_ref[...] * w_scale
    for j in range(IDX_HEADS):
        wb_ref[j * tq:(j + 1) * tq, :] = jnp.broadcast_to(w[:, j:j + 1], (tq, LANES))

    n_tiles = (i * tq + tq + tk - 1) // tk
    n_chunks = n_tiles * (tk // LANES)

    hp = IDX_PANEL_HEADS

    def tile(jk, carry):
        koff = pl.multiple_of(jk * tk, tk)
        kt = ik_ref[pl.ds(koff, tk), :]
        acc = jnp.zeros((tq, tk), jnp.float32)
        for pn in range(IDX_HEADS // hp):
            s = lax.dot_general(iqs_ref[pn * hp * tq:(pn + 1) * hp * tq, :], kt, _NT,
                                preferred_element_type=jnp.float32)
            for jj in range(hp):
                j = pn * hp + jj
                wj = wb_ref[j * tq:(j + 1) * tq, :]
                acc = acc + (jnp.maximum(s[jj * tq:(jj + 1) * tq, :], 0.0)
                             * jnp.concatenate([wj] * (tk // LANES), axis=1))
        qpos = i * tq + lax.broadcasted_iota(jnp.int32, (tq, tk), 0)
        kpos = koff + lax.broadcasted_iota(jnp.int32, (tq, tk), 1)
        acc = jnp.where(kpos <= qpos, acc, -jnp.inf)
        bits = pltpu.bitcast(acc, jnp.int32)
        key = bits ^ ((bits >> 31) & 0x7FFFFFFF)
        for part in range(tk // LANES):
            key_ref[jk * (tk // LANES) + part] = key[:, part * LANES:(part + 1) * LANES]
        return carry

    lax.fori_loop(0, n_tiles, tile, 0)

    def search(it, tau):
        cand = tau + jnp.left_shift(jnp.int32(1), 31 - it)

        def count(jk, cnt):
            for part in range(tk // LANES):
                cnt = cnt + jnp.where(key_ref[jk * (tk // LANES) + part] >= cand, 1.0, 0.0)
            return cnt

        cnt = lax.fori_loop(0, n_tiles, count, jnp.zeros((tq, LANES), jnp.float32))
        total = jnp.sum(cnt, axis=-1, keepdims=True)
        return jnp.where(total >= float(k_sel), cand, tau)

    tau = lax.fori_loop(0, 32, search, jnp.full((tq, LANES), -2 ** 31, jnp.int32))

    def emit(c, carry):
        key = key_ref[c]
        keep = (key >= tau) & (key > _KEY_NEG_INF)
        mask_ref[c] = jnp.where(keep, 0.0, NEG).astype(mask_ref.dtype)
        return carry

    lax.fori_loop(0, n_chunks, emit, 0)

    def fill(c, carry):
        mask_ref[c] = jnp.full((tq, LANES), NEG, mask_ref.dtype)
        return carry

    lax.fori_loop(n_chunks, n_chunks_total, fill, 0)


def _dsa_mask(p3, iw3, *, k_sel):
    bsz, seq, _ = p3.shape
    tq = IDX_TQ
    assert seq % IDX_TK == 0
    iq_block = 0
    ik_block = IDX_HEADS
    nck = seq // LANES
    w_scale = (IDX_HEADS ** -0.5) * (IDX_DIM ** -0.5)
    return pl.pallas_call(
        functools.partial(_indexer_body, k_sel=k_sel, w_scale=w_scale),
        grid=(bsz, seq // tq),
        in_specs=[
            pl.BlockSpec((None, tq, IDX_HEADS * IDX_DIM), lambda b, i: (b, i, iq_block)),
            pl.BlockSpec((None, seq, IDX_DIM), lambda b, i: (b, 0, ik_block)),
            pl.BlockSpec((None, tq, LANES), lambda b, i: (b, i, 0)),
        ],
        out_specs=pl.BlockSpec((None, nck, tq, LANES), lambda b, i: (b, 0, i, 0)),
        out_shape=jax.ShapeDtypeStruct((bsz, nck, seq, LANES), jnp.bfloat16),
        scratch_shapes=[
            pltpu.VMEM((IDX_HEADS * tq, IDX_DIM), jnp.bfloat16),
            pltpu.VMEM((IDX_HEADS * tq, LANES), jnp.float32),
            pltpu.VMEM((nck, tq, LANES), jnp.int32),
        ],
        compiler_params=_params("parallel", "arbitrary"),
        name="dsa_indexer",
    )(p3, p3, iw3)


def _attn_body(*refs, kind, n_back, seq):
    t = ATT_TILE
    rows = GROUP * t
    k = 0
    if kind == "swa":
        sink_ref = refs[k]; k += 1
    q_ref, k_ref, v_ref, at_ref = refs[k:k + 4]; k += 4
    if kind == "dsa":
        mask_ref = refs[k]; k += 1
    o_ref = refs[k]; k += 1
    qs_ref, m_ref, acc_ref, s0_ref, s1_ref = refs[k:k + 5]; k += 5
    if kind == "moba":
        kmean_ref = refs[k]

    h = pl.program_id(1)
    i = pl.program_id(2)
    n_delta = at_ref.shape[0]

    for g in range(GROUP):
        qs_ref[g * t:(g + 1) * t, :HEAD_DIM] = q_ref[:, g * HEAD_DIM:(g + 1) * HEAD_DIM]

    if kind == "swa":
        for g in range(GROUP):
            m_ref[g * t:(g + 1) * t, :] = jnp.full((t, LANES), sink_ref[h * GROUP + g] * LOG2E,
                                                   jnp.float32)
        acc_ref[:, :HEAD_DIM] = jnp.zeros((rows, HEAD_DIM), jnp.float32)
        acc_ref[:, HEAD_DIM:] = jnp.ones((rows, LANES), jnp.float32)
    else:
        m_ref[...] = jnp.full_like(m_ref, NEG)
        acc_ref[...] = jnp.zeros_like(acc_ref)

    if kind == "moba":
        n_blk = seq // MOBA_BLOCK

        @pl.when(i == 0)
        def _():
            kf = k_ref[...].astype(jnp.float32).reshape(n_blk, MOBA_BLOCK, HEAD_DIM)
            kmean_ref[...] = jnp.zeros_like(kmean_ref)
            kmean_ref[:n_blk, :] = jnp.sum(kf, axis=1) * (1.0 / MOBA_BLOCK)

        nb = -(-n_blk // 8) * 8
        km = kmean_ref[:nb, :]
        km_hi = km.astype(jnp.bfloat16)
        km_lo = (km - km_hi.astype(jnp.float32)).astype(jnp.bfloat16)
        q_all = qs_ref[:, :HEAD_DIM]
        gate = (lax.dot_general(km_hi, q_all, _NT, preferred_element_type=jnp.float32)
                + lax.dot_general(km_lo, q_all, _NT, preferred_element_type=jnp.float32))
        blk = lax.broadcasted_iota(jnp.int32, (nb, rows), 0).astype(jnp.float32)
        i_f = i.astype(jnp.float32)
        gate = jnp.where(blk < i_f, gate, -jnp.inf)
        picked = blk == i_f
        for _ in range(MOBA_TOPK):
            best = jnp.max(gate, axis=0, keepdims=True)
            first = jnp.min(jnp.where(gate == best, blk, float(LANES)), axis=0, keepdims=True)
            hit = (blk == first) & (best > -jnp.inf)
            picked = picked | hit
            gate = jnp.where(blk == first, -jnp.inf, gate)
        sel = jnp.concatenate([jnp.where(picked, 0.0, NEG),
                               jnp.full((LANES - nb, rows), NEG, jnp.float32)], axis=0)
        qs_ref[:, HEAD_DIM:] = sel.T.astype(jnp.bfloat16)

    def run_tiles(lo, hi, tk, use_at):
        ones = jnp.ones((tk, HEAD_DIM), jnp.bfloat16)

        def logits(j, dst_ref):
            koff = pl.multiple_of(j * tk, tk)
            kt = k_ref[pl.ds(koff, tk), :]
            if kind == "moba":
                lane = lax.broadcasted_iota(jnp.int32, (tk, LANES), 1)
                key = lax.broadcasted_iota(jnp.int32, (tk, LANES), 0)
                blk = j * (tk // MOBA_BLOCK) + key // MOBA_BLOCK
                kt = jnp.concatenate([kt, jnp.where(lane == blk, 1.0, 0.0).astype(jnp.bfloat16)], axis=1)
            dst_ref[:, :tk] = lax.dot_general(qs_ref[...], kt, _NT, preferred_element_type=jnp.float32)

        def softmax_pv(j, src_ref):
            koff = pl.multiple_of(j * tk, tk)
            vt = jnp.concatenate([v_ref[pl.ds(koff, tk), :], ones], axis=1)
            if use_at:
                d = jnp.minimum(i - j, n_delta - 1)
            if kind == "dsa":
                sel = jnp.concatenate([mask_ref[j * (tk // LANES) + part] for part in range(tk // LANES)],
                                      axis=1).astype(jnp.float32)
            for g in range(GROUP):
                r = slice(g * t, (g + 1) * t)
                s = src_ref[r, :tk]
                if use_at:
                    s = s + at_ref[d, r, :]
                if kind == "dsa":
                    s = s + sel
                m_prev = m_ref[r, :]
                m_new = jnp.maximum(m_prev, jnp.max(s, axis=-1, keepdims=True))
                alpha = jnp.exp2(m_prev - m_new)
                p = jnp.exp2(s - jnp.concatenate([m_new] * (tk // LANES), axis=1))
                m_ref[r, :] = m_new
                acc_ref[r, :] = (acc_ref[r, :] * jnp.concatenate([alpha, alpha], axis=1)
                                 + jnp.dot(p.astype(jnp.bfloat16), vt, preferred_element_type=jnp.float32))

        @pl.when(hi > lo)
        def _():
            logits(lo, s0_ref)

        n_pairs = (hi - lo) // 2

        def pair(pi, carry):
            j = lo + 2 * pi
            logits(j + 1, s1_ref)
            softmax_pv(j, s0_ref)
            logits(jnp.minimum(j + 2, hi - 1), s0_ref)
            softmax_pv(j + 1, s1_ref)
            return carry

        lax.fori_loop(0, n_pairs, pair, 0)

        @pl.when(lo + 2 * n_pairs < hi)
        def _():
            softmax_pv(hi - 1, s0_ref)

    if n_back is None:
        n_far = jnp.maximum(i - (n_delta - 2), 0) // 2
        run_tiles(0, n_far, 2 * t, False)
        for g in range(GROUP):
            r = slice(g * t, (g + 1) * t)
            m_ref[r, :] = m_ref[r, :] + at_ref[n_delta - 1, r, :LANES]
        lo = 2 * n_far
    else:
        lo = jnp.maximum(i - n_back, 0)
    run_tiles(lo, i + 1, t, True)

    for g in range(GROUP):
        r = slice(g * t, (g + 1) * t)
        out = acc_ref[r, :HEAD_DIM] / acc_ref[r, HEAD_DIM:]
        o_ref[:, g * HEAD_DIM:(g + 1) * HEAD_DIM] = out.astype(o_ref.dtype)


def _attention(p3, tiles, *, kind, n_heads, n_kv, sinks=None, mask=None):
    bsz, seq, _ = p3.shape
    t = ATT_TILE
    assert seq % t == 0
    rows = GROUP * t
    n_delta = tiles.shape[1]
    n_back = {"swa": n_delta - 1, "dil": n_delta - 1, "dsa": None, "moba": None}[kind]
    qw = GROUP * HEAD_DIM

    in_specs, args = [], []
    if kind == "swa":
        in_specs.append(pl.BlockSpec(memory_space=pltpu.SMEM))
        args.append(sinks.astype(jnp.float32))
    in_specs += [
        pl.BlockSpec((None, t, qw), lambda b, h, i: (b, i, h)),
        pl.BlockSpec((None, seq, HEAD_DIM), lambda b, h, i: (b, 0, n_heads + h)),
        pl.BlockSpec((None, seq, HEAD_DIM), lambda b, h, i: (b, 0, n_heads + n_kv + h)),
        pl.BlockSpec((None, n_delta, rows, t), lambda b, h, i: (h, 0, 0, 0),
                     pipeline_mode=pl.Buffered(1)),
    ]
    args += [p3, p3, p3, tiles]
    if kind == "dsa":
        in_specs.append(pl.BlockSpec((None, seq // LANES, t, LANES), lambda b, h, i: (b, 0, i, 0)))
        args.append(mask)
    q_cols = HEAD_DIM + (LANES if kind == "moba" else 0)
    s_cols = t if n_back is not None else 2 * t
    scratch = [
        pltpu.VMEM((rows, q_cols), jnp.bfloat16),
        pltpu.VMEM((rows, LANES), jnp.float32),
        pltpu.VMEM((rows, HEAD_DIM + LANES), jnp.float32),
        pltpu.VMEM((rows, s_cols), jnp.float32),
        pltpu.VMEM((rows, s_cols), jnp.float32),
    ]
    if kind == "moba":
        assert t == MOBA_BLOCK and seq // MOBA_BLOCK <= LANES
        scratch += [pltpu.VMEM((LANES, HEAD_DIM), jnp.float32)]
    return pl.pallas_call(
        functools.partial(_attn_body, kind=kind, n_back=n_back, seq=seq),
        grid=(bsz, n_kv, seq // t),
        in_specs=in_specs,
        out_specs=pl.BlockSpec((None, t, qw), lambda b, h, i: (b, i, h)),
        out_shape=jax.ShapeDtypeStruct((bsz, seq, n_heads * HEAD_DIM), jnp.bfloat16),
        scratch_shapes=scratch,
        compiler_params=_params("parallel", "parallel", "arbitrary"),
        name="attn_" + kind,
    )(*args)


def _pad_cols(w, mult):
    pad = (-w.shape[-1]) % mult
    if pad:
        w = jnp.pad(w, [(0, 0)] * (w.ndim - 1) + [(0, pad)])
    return w


def kernel(x, rel_bias_table, attn_norm, ffn_norm, final_norm, a_w_in, a_w_out, b_w_in, b_b_in, b_sinks, b_w_out, b_b_out, c_w_in, c_w_out, d_w_in, d_w_out, ffn_w_gate, ffn_w_up, ffn_conv_w, ffn_conv_b, ffn_w_down):
    bsz, seq, d_model = x.shape
    n = bsz * seq
    n_heads = d_model // HEAD_DIM
    n_kv = n_heads // GROUP
    q_w = n_heads * HEAD_DIM
    qkv_w = q_w + 2 * n_kv * HEAD_DIM
    depth = attn_norm.shape[0]
    bf = jnp.bfloat16
    tm = 1024 if n % 1024 == 0 else 512
    tn = 512
    ffn_tf = 512

    causal_t, swa_t, dil_t = _mixer_tiles(rel_bias_table.astype(jnp.float32), n_kv)

    q_fold = jnp.where(jnp.arange(qkv_w) < q_w, (HEAD_DIM ** -0.5) * LOG2E, 1.0).astype(jnp.float32)

    def in_proj(x2, gain, w, bias=None, col_scale=None):
        if col_scale is not None:
            w = w * col_scale
            bias = None if bias is None else bias * col_scale
        w = _pad_cols(w.astype(bf), tn)
        if bias is not None:
            bias = _pad_cols(bias, tn)
        return _matmul(x2, w, bias, out_dtype=bf, tm=tm, tn=tn, norm_gain=gain)

    x2 = x.reshape(n, d_model)
    for li in range(depth):
        m, j = li % 4, li // 4
        gain = attn_norm[li]
        if m == 0:
            main_w = qkv_w + IDX_HEADS * IDX_DIM + IDX_DIM
            p = in_proj(x2, gain, a_w_in[j][:, :qkv_w], col_scale=q_fold)
            p_idx = in_proj(x2, gain, a_w_in[j][:, qkv_w:main_w])
            iw = _matmul(x2, _pad_cols(a_w_in[j][:, main_w:].astype(bf), LANES),
                         out_dtype=jnp.float32, tm=tm, tn=LANES, norm_gain=gain)
            p3 = p.reshape(bsz, seq, -1)
            mask = _dsa_mask(p_idx.reshape(bsz, seq, -1), iw.reshape(bsz, seq, LANES),
                             k_sel=min(DSA_TOPK, seq // 4))
            o = _attention(p3, causal_t, kind="dsa", n_heads=n_heads, n_kv=n_kv, mask=mask)
            w_out, b_out = a_w_out[j], None
        elif m == 1:
            p = in_proj(x2, gain, b_w_in[j], b_b_in[j], col_scale=q_fold)
            o = _attention(p.reshape(bsz, seq, -1), swa_t, kind="swa", n_heads=n_heads, n_kv=n_kv,
                           sinks=b_sinks[j])
            w_out, b_out = b_w_out[j], b_b_out[j]
        elif m == 2:
            p = in_proj(x2, gain, c_w_in[j], col_scale=q_fold)
            o = _attention(p.reshape(bsz, seq, -1), causal_t, kind="moba", n_heads=n_heads, n_kv=n_kv)
            w_out, b_out = c_w_out[j], None
        else:
            p = in_proj(x2, gain, d_w_in[j], col_scale=q_fold)
            o = _attention(p.reshape(bsz, seq, -1), dil_t, kind="dil", n_heads=n_heads, n_kv=n_kv)
            w_out, b_out = d_w_out[j], None
        x2 = _matmul(o.reshape(n, q_w), w_out.astype(bf), b_out, x2,
                     out_dtype=jnp.float32, tm=tm, tn=tn)

        wg = _pad_cols(ffn_w_gate[li].astype(bf), ffn_tf)
        wu = _pad_cols(ffn_w_up[li].astype(bf), ffn_tf)
        cw = _pad_cols(ffn_conv_w[li], ffn_tf)
        cb = _pad_cols(ffn_conv_b[li].reshape(1, -1), ffn_tf)
        wd = ffn_w_down[li].astype(bf)
        pad = wg.shape[1] - wd.shape[0]
        if pad:
            wd = jnp.pad(wd, ((0, pad), (0, 0)))
        act = _ffn_act(x2, ffn_norm[li], wg, wu, cw, cb, seq=seq, tm=min(tm, seq), tf=ffn_tf)
        x2 = _matmul(act, wd, None, x2, out_dtype=jnp.float32, tm=tm, tn=256, single_buffer_lhs=True)

    out = _rmsnorm(x2, final_norm, jnp.float32)
    return out.reshape(bsz, seq, d_model)
```

```python
import functools
import math

import jax
import jax.numpy as jnp
from jax import lax
from jax.experimental import pallas as pl
from jax.experimental.pallas import tpu as pltpu

HEAD_DIM = 128
GROUP = 4
T5_BUCKETS = 32
T5_MAX_DIST = 2048
DSA_TOPK = 256
IDX_HEADS = 32
IDX_DIM = 128
SWA_WINDOW = 128
MOBA_BLOCK = 256
MOBA_TOPK = 3
DILATED_BRANCHES = ((128, 1), (512, 4), (2048, 16))
RMS_EPS = 1e-6

LANES = 128
NEG = -1e30
LOG2E = math.log2(math.e)
ATT_TILE = 256
IDX_TQ = 128
IDX_TK = 256
IDX_PANEL_HEADS = 4
VMEM_LIMIT = 56 * 1024 * 1024

_NT = (((1,), (1,)), ((), ()))


def _params(*sem):
    return pltpu.CompilerParams(dimension_semantics=sem, vmem_limit_bytes=VMEM_LIMIT)


def _rmsnorm_body(x_ref, g_ref, o_ref):
    x = x_ref[...]
    ms = jnp.mean(x * x, axis=-1, keepdims=True)
    y = x * lax.rsqrt(ms + RMS_EPS)
    o_ref[...] = (y * g_ref[...]).astype(o_ref.dtype)


def _rmsnorm(x2d, gain, out_dtype):
    n, d = x2d.shape
    tm = 256
    return pl.pallas_call(
        _rmsnorm_body,
        grid=(n // tm,),
        in_specs=[pl.BlockSpec((tm, d), lambda i: (i, 0)),
                  pl.BlockSpec((1, d), lambda i: (0, 0))],
        out_specs=pl.BlockSpec((tm, d), lambda i: (i, 0)),
        out_shape=jax.ShapeDtypeStruct((n, d), out_dtype),
        compiler_params=_params("parallel"),
        name="rmsnorm",
    )(x2d, gain.reshape(1, d))


NORM_ROWS = 16


def _rms_rows(x_ref, g_ref, dst_ref, dst_row0, n_rows):
    def chunk(c, carry):
        r0 = pl.multiple_of(c * NORM_ROWS, NORM_ROWS)
        x = x_ref[pl.ds(r0, NORM_ROWS), :]
        ms = jnp.mean(x * x, axis=-1, keepdims=True)
        y = x * lax.rsqrt(ms + RMS_EPS)
        dst_ref[pl.ds(dst_row0 + r0, NORM_ROWS), :] = (y * g_ref[...]).astype(dst_ref.dtype)
        return carry

    lax.fori_loop(0, n_rows // NORM_ROWS, chunk, 0, unroll=4)


def _matmul_body(*refs, has_norm, has_bias, has_res, tm):
    k = 0
    if has_norm:
        x_ref, g_ref = refs[0], refs[1]
        hs_ref = refs[-1]
        o_ref = refs[-2]
        k = 2

        @pl.when(pl.program_id(1) == 0)
        def _():
            _rms_rows(x_ref, g_ref, hs_ref, 0, tm)

        a = hs_ref[...]
    else:
        a = refs[0][...]
        o_ref = refs[-1]
        k = 1
    acc = jnp.dot(a, refs[k][...], preferred_element_type=jnp.float32)
    k += 1
    if has_bias:
        acc = acc + refs[k][...]
        k += 1
    if has_res:
        acc = acc + refs[k][...]
    o_ref[...] = acc.astype(o_ref.dtype)


def _matmul(a, w, bias=None, res=None, *, out_dtype, tm, tn, single_buffer_lhs=False, norm_gain=None):
    n, kdim = a.shape
    wn = w.shape[1]
    assert n % tm == 0 and wn % tn == 0 and tm % NORM_ROWS == 0
    has_norm = norm_gain is not None
    lhs_mode = {"pipeline_mode": pl.Buffered(1)} if single_buffer_lhs else {}
    in_specs = [pl.BlockSpec((tm, kdim), lambda i, j: (i, 0), **lhs_mode)]
    args = [a]
    if has_norm:
        in_specs.append(pl.BlockSpec((1, kdim), lambda i, j: (0, 0)))
        args.append(norm_gain.reshape(1, kdim))
    in_specs.append(pl.BlockSpec((kdim, tn), lambda i, j: (0, j)))
    args.append(w)
    if bias is not None:
        in_specs.append(pl.BlockSpec((1, tn), lambda i, j: (0, j)))
        args.append(bias.reshape(1, wn).astype(jnp.float32))
    if res is not None:
        in_specs.append(pl.BlockSpec((tm, tn), lambda i, j: (i, j)))
        args.append(res)
    return pl.pallas_call(
        functools.partial(_matmul_body, has_norm=has_norm, has_bias=bias is not None,
                          has_res=res is not None, tm=tm),
        grid=(n // tm, wn // tn),
        in_specs=in_specs,
        out_specs=pl.BlockSpec((tm, tn), lambda i, j: (i, j)),
        out_shape=jax.ShapeDtypeStruct((n, wn), out_dtype),
        scratch_shapes=[pltpu.VMEM((tm, kdim), jnp.bfloat16)] if has_norm else [],
        compiler_params=_params("parallel", "arbitrary"),
        name="matmul",
    )(*args)


FFN_HALO = 16


def _ffn_act_body(x_ref, xp_ref, gain_ref, wg_ref, wu_ref, cw_ref, cb_ref, o_ref, hs_ref, *, tm, seq):
    i = pl.program_id(0)
    f = pl.program_id(1)

    @pl.when(f == 0)
    def _():
        _rms_rows(x_ref, gain_ref, hs_ref, FFN_HALO, tm)
        xp = xp_ref[...]
        hp = xp * lax.rsqrt(jnp.mean(xp * xp, axis=-1, keepdims=True) + RMS_EPS) * gain_ref[...]
        first = (i * tm) % seq == 0
        hs_ref[:FFN_HALO, :] = jnp.where(first, jnp.zeros_like(hp), hp).astype(hs_ref.dtype)

    g = jnp.dot(hs_ref[...], wg_ref[...], preferred_element_type=jnp.float32)
    u = jnp.dot(hs_ref[FFN_HALO:, :], wu_ref[...], preferred_element_type=jnp.float32)
    cw = cw_ref[...]
    y = (cb_ref[...]
         + cw[0:1, :] * g[FFN_HALO - 2:FFN_HALO - 2 + tm, :]
         + cw[1:2, :] * g[FFN_HALO - 1:FFN_HALO - 1 + tm, :]
         + cw[2:3, :] * g[FFN_HALO:, :])
    o_ref[...] = ((y * (1.0 / (1.0 + jnp.exp(-y)))) * u).astype(o_ref.dtype)


def _ffn_act(x, gain, wg, wu, cw, cb, *, seq, tm, tf):
    n, d = x.shape
    fdim = wg.shape[1]
    assert n % tm == 0 and seq % tm == 0 and tm % FFN_HALO == 0
    hb = tm // FFN_HALO
    return pl.pallas_call(
        functools.partial(_ffn_act_body, tm=tm, seq=seq),
        grid=(n // tm, pl.cdiv(fdim, tf)),
        in_specs=[
            pl.BlockSpec((tm, d), lambda i, f: (i, 0), pipeline_mode=pl.Buffered(1)),
            pl.BlockSpec((FFN_HALO, d), lambda i, f: (jnp.maximum(i * hb - 1, 0), 0)),
            pl.BlockSpec((1, d), lambda i, f: (0, 0)),
            pl.BlockSpec((d, tf), lambda i, f: (0, f)),
            pl.BlockSpec((d, tf), lambda i, f: (0, f)),
            pl.BlockSpec((3, tf), lambda i, f: (0, f)),
            pl.BlockSpec((1, tf), lambda i, f: (0, f)),
        ],
        out_specs=pl.BlockSpec((tm, tf), lambda i, f: (i, f)),
        out_shape=jax.ShapeDtypeStruct((n, fdim), jnp.bfloat16),
        scratch_shapes=[pltpu.VMEM((FFN_HALO + tm, d), jnp.bfloat16)],
        compiler_params=_params("parallel", "arbitrary"),
        name="ffn_act",
    )(x, x, gain.reshape(1, d), wg, wu, cw, cb)


def _t5_bucket(dist):
    n = jnp.maximum(dist, 0)
    max_exact = T5_BUCKETS // 2
    nf = jnp.maximum(n, 1).astype(jnp.float32)
    large = max_exact + (jnp.log(nf / max_exact) / math.log(T5_MAX_DIST / max_exact)
                         * (T5_BUCKETS - max_exact)).astype(jnp.int32)
    large = jnp.minimum(large, T5_BUCKETS - 1)
    return jnp.where(n < max_exact, n, large)


def _tile_dist(n_delta):
    t = ATT_TILE
    d = jnp.arange(n_delta, dtype=jnp.int32)[:, None, None] * t
    qi = jnp.arange(t, dtype=jnp.int32)[None, :, None]
    kj = jnp.arange(t, dtype=jnp.int32)[None, None, :]
    return d + qi - kj


def _at_body(tab_ref, bkt_ref, add_ref, o_ref, *, n_heads):
    head0 = pl.program_id(0) * GROUP
    bkt = bkt_ref[...]
    vals = [jnp.zeros(bkt.shape, jnp.float32) for _ in range(GROUP)]
    for b in range(T5_BUCKETS):
        hit = bkt == b
        vals = [jnp.where(hit, tab_ref[b * n_heads + head0 + g], vals[g]) for g in range(GROUP)]
    add = add_ref[...]
    for g in range(GROUP):
        o_ref[g] = (vals[g] + add) * LOG2E


def _additive_tiles(table, bucket, add, n_kv):
    n_delta, t, _ = bucket.shape
    n_heads = n_kv * GROUP
    out = pl.pallas_call(
        functools.partial(_at_body, n_heads=n_heads),
        grid=(n_kv, n_delta),
        in_specs=[pl.BlockSpec(memory_space=pltpu.SMEM),
                  pl.BlockSpec((None, t, t), lambda h, d: (d, 0, 0)),
                  pl.BlockSpec((None, t, t), lambda h, d: (d, 0, 0))],
        out_specs=pl.BlockSpec((None, None, GROUP, t, t), lambda h, d: (h, d, 0, 0, 0)),
        out_shape=jax.ShapeDtypeStruct((n_kv, n_delta, GROUP, t, t), jnp.float32),
        compiler_params=_params("parallel", "parallel"),
        name="bias_tiles",
    )(table.reshape(-1), bucket, add)
    return out.reshape(n_kv, n_delta, GROUP * t, t)


def _mixer_tiles(table, n_kv):
    max_exact = T5_BUCKETS // 2
    far = math.ceil(max_exact * (T5_MAX_DIST / max_exact)
                    ** ((T5_BUCKETS - 1 - max_exact) / (T5_BUCKETS - max_exact))) + 2
    nd_causal = -(-(far + ATT_TILE - 1) // ATT_TILE) + 1
    dist = _tile_dist(nd_causal)
    causal = _additive_tiles(table, _t5_bucket(dist),
                             jnp.where(dist >= 0, 0.0, NEG).astype(jnp.float32), n_kv)

    nd_swa = (SWA_WINDOW - 1 + ATT_TILE - 1) // ATT_TILE + 1
    dist = _tile_dist(nd_swa)
    swa = _additive_tiles(table, _t5_bucket(dist),
                          jnp.where((dist >= 0) & (dist < SWA_WINDOW), 0.0, NEG).astype(jnp.float32),
                          n_kv)

    max_w = max(w for w, _ in DILATED_BRANCHES)
    nd_dil = (max_w + ATT_TILE - 1) // ATT_TILE + 1
    dist = _tile_dist(nd_dil)
    count = jnp.zeros(dist.shape, jnp.int32)
    for w, r in DILATED_BRANCHES:
        count = count + ((dist >= 0) & (dist <= w) & (dist % r == 0)).astype(jnp.int32)
    add = jnp.where(count > 0, jnp.log(jnp.maximum(count, 1).astype(jnp.float32)), NEG)
    dil = _additive_tiles(table, _t5_bucket(dist), add.astype(jnp.float32), n_kv)
    return causal, swa, dil


_KEY_NEG_INF = -2139095041


def _indexer_body(iq_ref, ik_ref, iw_ref, mask_ref, iqs_ref, wb_ref, key_ref, *, k_sel, w_scale):
    tq, tk = IDX_TQ, IDX_TK
    i = pl.program_id(1)
    n_chunks_total = key_ref.shape[0]

    for j in range(IDX_HEADS):
        iqs_ref[j * tq:(j + 1) * tq, :] = iq_ref[:, j * IDX_DIM:(j + 1) * IDX_DIM]
    w = iw_ref[...] * w_scale
    for j in range(IDX_HEADS):
        wb_ref[j * tq:(j + 1) * tq, :] = jnp.broadcast_to(w[:, j:j + 1], (tq, LANES))

    n_tiles = (i * tq + tq + tk - 1) // tk
    n_chunks = n_tiles * (tk // LANES)

    hp = IDX_PANEL_HEADS

    def tile(jk, carry):
        koff = pl.multiple_of(jk * tk, tk)
        kt = ik_ref[pl.ds(koff, tk), :]
        acc = jnp.zeros((tq, tk), jnp.float32)
        for pn in range(IDX_HEADS // hp):
            s = lax.dot_general(iqs_ref[pn * hp * tq:(pn + 1) * hp * tq, :], kt, _NT,
                                preferred_element_type=jnp.float32)
            for jj in range(hp):
                j = pn * hp + jj
                wj = wb_ref[j * tq:(j + 1) * tq, :]
                acc = acc + (jnp.maximum(s[jj * tq:(jj + 1) * tq, :], 0.0)
                             * jnp.concatenate([wj] * (tk // LANES), axis=1))
        qpos = i * tq + lax.broadcasted_iota(jnp.int32, (tq, tk), 0)
        kpos = koff + lax.broadcasted_iota(jnp.int32, (tq, tk), 1)
        acc = jnp.where(kpos <= qpos, acc, -jnp.inf)
        bits = pltpu.bitcast(acc, jnp.int32)
        key = bits ^ ((bits >> 31) & 0x7FFFFFFF)
        for part in range(tk // LANES):
            key_ref[jk * (tk // LANES) + part] = key[:, part * LANES:(part + 1) * LANES]
        return carry

    lax.fori_loop(0, n_tiles, tile, 0)

    def search(it, tau):
        cand = tau + jnp.left_shift(jnp.int32(1), 31 - it)

        def count(jk, cnt):
            for part in range(tk // LANES):
                cnt = cnt + jnp.where(key_ref[jk * (tk // LANES) + part] >= cand, 1.0, 0.0)
            return cnt

        cnt = lax.fori_loop(0, n_tiles, count, jnp.zeros((tq, LANES), jnp.float32))
        total = jnp.sum(cnt, axis=-1, keepdims=True)
        return jnp.where(total >= float(k_sel), cand, tau)

    tau = lax.fori_loop(0, 32, search, jnp.full((tq, LANES), -2 ** 31, jnp.int32))

    def emit(c, carry):
        key = key_ref[c]
        keep = (key >= tau) & (key > _KEY_NEG_INF)
        mask_ref[c] = jnp.where(keep, 0.0, NEG).astype(mask_ref.dtype)
        return carry

    lax.fori_loop(0, n_chunks, emit, 0)

    def fill(c, carry):
        mask_ref[c] = jnp.full((tq, LANES), NEG, mask_ref.dtype)
        return carry

    lax.fori_loop(n_chunks, n_chunks_total, fill, 0)


def _dsa_mask(p3, iw3, *, k_sel):
    bsz, seq, _ = p3.shape
    tq = IDX_TQ
    assert seq % IDX_TK == 0
    iq_block = 0
    ik_block = IDX_HEADS
    nck = seq // LANES
    w_scale = (IDX_HEADS ** -0.5) * (IDX_DIM ** -0.5)
    return pl.pallas_call(
        functools.partial(_indexer_body, k_sel=k_sel, w_scale=w_scale),
        grid=(bsz, seq // tq),
        in_specs=[
            pl.BlockSpec((None, tq, IDX_HEADS * IDX_DIM), lambda b, i: (b, i, iq_block)),
            pl.BlockSpec((None, seq, IDX_DIM), lambda b, i: (b, 0, ik_block)),
            pl.BlockSpec((None, tq, LANES), lambda b, i: (b, i, 0)),
        ],
        out_specs=pl.BlockSpec((None, nck, tq, LANES), lambda b, i: (b, 0, i, 0)),
        out_shape=jax.ShapeDtypeStruct((bsz, nck, seq, LANES), jnp.bfloat16),
        scratch_shapes=[
            pltpu.VMEM((IDX_HEADS * tq, IDX_DIM), jnp.bfloat16),
            pltpu.VMEM((IDX_HEADS * tq, LANES), jnp.float32),
            pltpu.VMEM((nck, tq, LANES), jnp.int32),
        ],
        compiler_params=_params("parallel", "arbitrary"),
        name="dsa_indexer",
    )(p3, p3, iw3)


def _attn_body(*refs, kind, n_back, seq):
    t = ATT_TILE
    rows = GROUP * t
    k = 0
    if kind == "swa":
        sink_ref = refs[k]; k += 1
    q_ref, k_ref, v_ref, at_ref = refs[k:k + 4]; k += 4
    if kind == "dsa":
        mask_ref = refs[k]; k += 1
    o_ref = refs[k]; k += 1
    qs_ref, m_ref, acc_ref, s0_ref, s1_ref = refs[k:k + 5]; k += 5
    if kind == "moba":
        kmean_ref = refs[k]

    h = pl.program_id(1)
    i = pl.program_id(2)
    n_delta = at_ref.shape[0]

    for g in range(GROUP):
        qs_ref[g * t:(g + 1) * t, :HEAD_DIM] = q_ref[:, g * HEAD_DIM:(g + 1) * HEAD_DIM]

    if kind == "swa":
        for g in range(GROUP):
            m_ref[g * t:(g + 1) * t, :] = jnp.full((t, LANES), sink_ref[h * GROUP + g] * LOG2E,
                                                   jnp.float32)
        acc_ref[:, :HEAD_DIM] = jnp.zeros((rows, HEAD_DIM), jnp.float32)
        acc_ref[:, HEAD_DIM:] = jnp.ones((rows, LANES), jnp.float32)
    else:
        m_ref[...] = jnp.full_like(m_ref, NEG)
        acc_ref[...] = jnp.zeros_like(acc_ref)

    if kind == "moba":
        n_blk = seq // MOBA_BLOCK

        @pl.when(i == 0)
        def _():
            kf = k_ref[...].astype(jnp.float32).reshape(n_blk, MOBA_BLOCK, HEAD_DIM)
            kmean_ref[...] = jnp.zeros_like(kmean_ref)
            kmean_ref[:n_blk, :] = jnp.sum(kf, axis=1) * (1.0 / MOBA_BLOCK)

        nb = -(-n_blk // 8) * 8
        km = kmean_ref[:nb, :]
        km_hi = km.astype(jnp.bfloat16)
        km_lo = (km - km_hi.astype(jnp.float32)).astype(jnp.bfloat16)
        q_all = qs_ref[:, :HEAD_DIM]
        gate = (lax.dot_general(km_hi, q_all, _NT, preferred_element_type=jnp.float32)
                + lax.dot_general(km_lo, q_all, _NT, preferred_element_type=jnp.float32))
        blk = lax.broadcasted_iota(jnp.int32, (nb, rows), 0).astype(jnp.float32)
        i_f = i.astype(jnp.float32)
        gate = jnp.where(blk < i_f, gate, -jnp.inf)
        picked = blk == i_f
        for _ in range(MOBA_TOPK):
            best = jnp.max(gate, axis=0, keepdims=True)
            first = jnp.min(jnp.where(gate == best, blk, float(LANES)), axis=0, keepdims=True)
            hit = (blk == first) & (best > -jnp.inf)
            picked = picked | hit
            gate = jnp.where(blk == first, -jnp.inf, gate)
        sel = jnp.concatenate([jnp.where(picked, 0.0, NEG),
                               jnp.full((LANES - nb, rows), NEG, jnp.float32)], axis=0)
        qs_ref[:, HEAD_DIM:] = sel.T.astype(jnp.bfloat16)

    def run_tiles(lo, hi, tk, use_at):
        ones = jnp.ones((tk, HEAD_DIM), jnp.bfloat16)

        def logits(j, dst_ref):
            koff = pl.multiple_of(j * tk, tk)
            kt = k_ref[pl.ds(koff, tk), :]
            if kind == "moba":
                lane = lax.broadcasted_iota(jnp.int32, (tk, LANES), 1)
                key = lax.broadcasted_iota(jnp.int32, (tk, LANES), 0)
                blk = j * (tk // MOBA_BLOCK) + key // MOBA_BLOCK
                kt = jnp.concatenate([kt, jnp.where(lane == blk, 1.0, 0.0).astype(jnp.bfloat16)], axis=1)
            dst_ref[:, :tk] = lax.dot_general(qs_ref[...], kt, _NT, preferred_element_type=jnp.float32)

        def softmax_pv(j, src_ref):
            koff = pl.multiple_of(j * tk, tk)
            vt = jnp.concatenate([v_ref[pl.ds(koff, tk), :], ones], axis=1)
            if use_at:
                d = jnp.minimum(i - j, n_delta - 1)
            if kind == "dsa":
                sel = jnp.concatenate([mask_ref[j * (tk // LANES) + part] for part in range(tk // LANES)],
                                      axis=1).astype(jnp.float32)
            for g in range(GROUP):
                r = slice(g * t, (g + 1) * t)
                s = src_ref[r, :tk]
                if use_at:
                    s = s + at_ref[d, r, :]
                if kind == "dsa":
                    s = s + sel
                m_prev = m_ref[r, :]
                m_new = jnp.maximum(m_prev, jnp.max(s, axis=-1, keepdims=True))
                alpha = jnp.exp2(m_prev - m_new)
                p = jnp.exp2(s - jnp.concatenate([m_new] * (tk // LANES), axis=1))
                m_ref[r, :] = m_new
                acc_ref[r, :] = (acc_ref[r, :] * jnp.concatenate([alpha, alpha], axis=1)
                                 + jnp.dot(p.astype(jnp.bfloat16), vt, preferred_element_type=jnp.float32))

        @pl.when(hi > lo)
        def _():
            logits(lo, s0_ref)

        n_pairs = (hi - lo) // 2

        def pair(pi, carry):
            j = lo + 2 * pi
            logits(j + 1, s1_ref)
            softmax_pv(j, s0_ref)
            logits(jnp.minimum(j + 2, hi - 1), s0_ref)
            softmax_pv(j + 1, s1_ref)
            return carry

        lax.fori_loop(0, n_pairs, pair, 0)

        @pl.when(lo + 2 * n_pairs < hi)
        def _():
            softmax_pv(hi - 1, s0_ref)

    if n_back is None:
        n_far = jnp.maximum(i - (n_delta - 2), 0) // 2
        run_tiles(0, n_far, 2 * t, False)
        for g in range(GROUP):
            r = slice(g * t, (g + 1) * t)
            m_ref[r, :] = m_ref[r, :] + at_ref[n_delta - 1, r, :LANES]
        lo = 2 * n_far
    else:
        lo = jnp.maximum(i - n_back, 0)
    run_tiles(lo, i + 1, t, True)

    for g in range(GROUP):
        r = slice(g * t, (g + 1) * t)
        out = acc_ref[r, :HEAD_DIM] / acc_ref[r, HEAD_DIM:]
        o_ref[:, g * HEAD_DIM:(g + 1) * HEAD_DIM] = out.astype(o_ref.dtype)


def _attention(p3, tiles, *, kind, n_heads, n_kv, sinks=None, mask=None):
    bsz, seq, _ = p3.shape
    t = ATT_TILE
    assert seq % t == 0
    rows = GROUP * t
    n_delta = tiles.shape[1]
    n_back = {"swa": n_delta - 1, "dil": n_delta - 1, "dsa": None, "moba": None}[kind]
    qw = GROUP * HEAD_DIM

    in_specs, args = [], []
    if kind == "swa":
        in_specs.append(pl.BlockSpec(memory_space=pltpu.SMEM))
        args.append(sinks.astype(jnp.float32))
    in_specs += [
        pl.BlockSpec((None, t, qw), lambda b, h, i: (b, i, h)),
        pl.BlockSpec((None, seq, HEAD_DIM), lambda b, h, i: (b, 0, n_heads + h)),
        pl.BlockSpec((None, seq, HEAD_DIM), lambda b, h, i: (b, 0, n_heads + n_kv + h)),
        pl.BlockSpec((None, n_delta, rows, t), lambda b, h, i: (h, 0, 0, 0),
                     pipeline_mode=pl.Buffered(1)),
    ]
    args += [p3, p3, p3, tiles]
    if kind == "dsa":
        in_specs.append(pl.BlockSpec((None, seq // LANES, t, LANES), lambda b, h, i: (b, 0, i, 0)))
        args.append(mask)
    q_cols = HEAD_DIM + (LANES if kind == "moba" else 0)
    s_cols = t if n_back is not None else 2 * t
    scratch = [
        pltpu.VMEM((rows, q_cols), jnp.bfloat16),
        pltpu.VMEM((rows, LANES), jnp.float32),
        pltpu.VMEM((rows, HEAD_DIM + LANES), jnp.float32),
        pltpu.VMEM((rows, s_cols), jnp.float32),
        pltpu.VMEM((rows, s_cols), jnp.float32),
    ]
    if kind == "moba":
        assert t == MOBA_BLOCK and seq // MOBA_BLOCK <= LANES
        scratch += [pltpu.VMEM((LANES, HEAD_DIM), jnp.float32)]
    return pl.pallas_call(
        functools.partial(_attn_body, kind=kind, n_back=n_back, seq=seq),
        grid=(bsz, n_kv, seq // t),
        in_specs=in_specs,
        out_specs=pl.BlockSpec((None, t, qw), lambda b, h, i: (b, i, h)),
        out_shape=jax.ShapeDtypeStruct((bsz, seq, n_heads * HEAD_DIM), jnp.bfloat16),
        scratch_shapes=scratch,
        compiler_params=_params("parallel", "parallel", "arbitrary"),
        name="attn_" + kind,
    )(*args)


def _pad_cols(w, mult):
    pad = (-w.shape[-1]) % mult
    if pad:
        w = jnp.pad(w, [(0, 0)] * (w.ndim - 1) + [(0, pad)])
    return w


def kernel(x, rel_bias_table, attn_norm, ffn_norm, final_norm, a_w_in, a_w_out, b_w_in, b_b_in, b_sinks, b_w_out, b_b_out, c_w_in, c_w_out, d_w_in, d_w_out, ffn_w_gate, ffn_w_up, ffn_conv_w, ffn_conv_b, ffn_w_down):
    bsz, seq, d_model = x.shape
    n = bsz * seq
    n_heads = d_model // HEAD_DIM
    n_kv = n_heads // GROUP
    q_w = n_heads * HEAD_DIM
    qkv_w = q_w + 2 * n_kv * HEAD_DIM
    depth = attn_norm.shape[0]
    bf = jnp.bfloat16
    tm = 1024 if n % 1024 == 0 else 512
    tn = 512
    ffn_tf = 512

    causal_t, swa_t, dil_t = _mixer_tiles(rel_bias_table.astype(jnp.float32), n_kv)

    q_fold = jnp.where(jnp.arange(qkv_w) < q_w, (HEAD_DIM ** -0.5) * LOG2E, 1.0).astype(jnp.float32)

    def in_proj(x2, gain, w, bias=None, col_scale=None):
        if col_scale is not None:
            w = w * col_scale
            bias = None if bias is None else bias * col_scale
        w = _pad_cols(w.astype(bf), tn)
        if bias is not None:
            bias = _pad_cols(bias, tn)
        return _matmul(x2, w, bias, out_dtype=bf, tm=tm, tn=tn, norm_gain=gain)

    x2 = x.reshape(n, d_model)
    for li in range(depth):
        m, j = li % 4, li // 4
        gain = attn_norm[li]
        if m == 0:
            main_w = qkv_w + IDX_HEADS * IDX_DIM + IDX_DIM
            p = in_proj(x2, gain, a_w_in[j][:, :qkv_w], col_scale=q_fold)
            p_idx = in_proj(x2, gain, a_w_in[j][:, qkv_w:main_w])
            iw = _matmul(x2, _pad_cols(a_w_in[j][:, main_w:].astype(bf), LANES),
                         out_dtype=jnp.float32, tm=tm, tn=LANES, norm_gain=gain)
            p3 = p.reshape(bsz, seq, -1)
            mask = _dsa_mask(p_idx.reshape(bsz, seq, -1), iw.reshape(bsz, seq, LANES),
                             k_sel=min(DSA_TOPK, seq // 4))
            o = _attention(p3, causal_t, kind="dsa", n_heads=n_heads, n_kv=n_kv, mask=mask)
            w_out, b_out = a_w_out[j], None
        elif m == 1:
            p = in_proj(x2, gain, b_w_in[j], b_b_in[j], col_scale=q_fold)
            o = _attention(p.reshape(bsz, seq, -1), swa_t, kind="swa", n_heads=n_heads, n_kv=n_kv,
                           sinks=b_sinks[j])
            w_out, b_out = b_w_out[j], b_b_out[j]
        elif m == 2:
            p = in_proj(x2, gain, c_w_in[j], col_scale=q_fold)
            o = _attention(p.reshape(bsz, seq, -1), causal_t, kind="moba", n_heads=n_heads, n_kv=n_kv)
            w_out, b_out = c_w_out[j], None
        else:
            p = in_proj(x2, gain, d_w_in[j], col_scale=q_fold)
            o = _attention(p.reshape(bsz, seq, -1), dil_t, kind="dil", n_heads=n_heads, n_kv=n_kv)
            w_out, b_out = d_w_out[j], None
        x2 = _matmul(o.reshape(n, q_w), w_out.astype(bf), b_out, x2,
                     out_dtype=jnp.float32, tm=tm, tn=tn)

        act = _ffn_act(x2, ffn_norm[li], ffn_w_gate[li].astype(bf), ffn_w_up[li].astype(bf),
                       ffn_conv_w[li], ffn_conv_b[li].reshape(1, -1), seq=seq, tm=min(tm, seq), tf=ffn_tf)
        x2 = _matmul(act, ffn_w_down[li].astype(bf), None, x2, out_dtype=jnp.float32, tm=tm, tn=256,
                     single_buffer_lhs=True)

    out = _rmsnorm(x2, final_norm, jnp.float32)
    return out.reshape(bsz, seq, d_model)
```
